```python
import jax, jax.numpy as jnp
from jax import lax
import numpy as np

D_MODEL = 1024
BATCH = 8
SEQ = 4096
DEPTH = 2

HEAD_DIM = 64
FOX_HEADS = 8
FOX_W = FOX_HEADS * HEAD_DIM
RWKV_HEADS = 4
RWKV_W = RWKV_HEADS * HEAD_DIM
POOL_GROUPS = 4
POOL_GROUP_DIM = 64
POOL_W = POOL_GROUPS * POOL_GROUP_DIM
POOL_WINDOWS = (2, 4, 8, 16)
D_MIX = FOX_W + RWKV_W + POOL_W
RWKV_DECAY_RANK = 64
RWKV_ICL_RANK = 64
RWKV_SHIFT_W = 3 * RWKV_W + RWKV_DECAY_RANK + RWKV_ICL_RANK
Q_BLOCK = 128
NORM_EPS = 1e-6
RWKV_GN_EPS = 64e-5
DECAY_SCALE = float(np.exp(-0.5))
IN_SPLITS = (FOX_W, FOX_W, FOX_W, FOX_HEADS, RWKV_SHIFT_W, POOL_W, D_MIX)
D_IN = int(sum(IN_SPLITS))
IN_IDX = [int(i) for i in np.cumsum(IN_SPLITS)[:-1]]
RWKV_IDX = [int(i) for i in np.cumsum((RWKV_W, RWKV_W, RWKV_W, RWKV_DECAY_RANK))]

kernel_name = "hymba_fox_rwkv7_pool_adaln"


def rms_norm(z, g):
    zf = z.astype(jnp.float32)
    y = zf * lax.rsqrt(jnp.mean(zf * zf, axis=-1, keepdims=True) + NORM_EPS)
    return (y * g.astype(jnp.float32)).astype(z.dtype)


def token_shift(z):
    return jnp.pad(z, ((0, 0), (1, 0), (0, 0)))[:, :-1]


def fox_attention(q, k, v, f_logit, q_gain, k_gain, f_bias):
    B, S, _ = q.shape

    def heads(z):
        return z.reshape(B, S, FOX_HEADS, HEAD_DIM).transpose(0, 2, 1, 3)

    qh = rms_norm(heads(q), q_gain).astype(jnp.float32) * (HEAD_DIM ** -0.5)
    kh = rms_norm(heads(k), k_gain).astype(jnp.float32)
    vh = heads(v).astype(jnp.float32)
    log_f = jax.nn.log_sigmoid((f_logit + f_bias).astype(jnp.float32))
    cum = jnp.cumsum(log_f, axis=1).transpose(0, 2, 1)
    outs = []
    for i in range(S // Q_BLOCK):
        q0 = i * Q_BLOCK
        L = q0 + Q_BLOCK
        s = jnp.einsum('bhqd,bhkd->bhqk', qh[:, :, q0:L], kh[:, :, :L])
        s = s + cum[:, :, q0:L, None] - cum[:, :, None, :L]
        causal = jnp.arange(L)[None, :] <= (q0 + jnp.arange(Q_BLOCK))[:, None]
        p = jax.nn.softmax(jnp.where(causal, s, -jnp.inf), axis=-1)
        outs.append(jnp.einsum('bhqk,bhkd->bhqd', p, vh[:, :, :L]))
    o = jnp.concatenate(outs, axis=2)
    return o.transpose(0, 2, 1, 3).reshape(B, S, FOX_W).astype(q.dtype)


def rwkv7_time_mix(z, mu, w0, w2, a0, a2, k_k, k_a, r_k, ln_g, ln_b):
    B, S, _ = z.shape
    zf = z.astype(jnp.float32)
    zf = zf + (token_shift(zf) - zf) * mu.astype(jnp.float32)
    r, k, v, wd, ad = jnp.split(zf, RWKV_IDX, axis=-1)
    log_w = -DECAY_SCALE * jax.nn.sigmoid(w0 + jnp.tanh(wd) @ w2)
    a = jax.nn.sigmoid(a0 + ad @ a2)
    kk = (k * k_k).reshape(B, S, RWKV_HEADS, HEAD_DIM)
    kk = kk / jnp.maximum(jnp.linalg.norm(kk, axis=-1, keepdims=True), 1e-12)
    k = k * (1.0 + (a - 1.0) * k_a)

    def hd(t):
        return t.reshape(B, S, RWKV_HEADS, HEAD_DIM)

    rh, kh, vh, ah = hd(r), hd(k), hd(v), hd(a)
    wh = jnp.exp(hd(log_w))
    xs = tuple(t.transpose(1, 0, 2, 3) for t in (rh, wh, kh, vh, kk, ah))

    def step(state, inp):
        r_t, w_t, k_t, v_t, kk_t, a_t = inp
        sa = jnp.einsum('bhij,bhj->bhi', state, -kk_t)
        state = (state * w_t[:, :, None, :]
                 + sa[..., :, None] * (kk_t * a_t)[:, :, None, :]
                 + v_t[..., :, None] * k_t[:, :, None, :])
        return state, jnp.einsum('bhij,bhj->bhi', state, r_t)

    state0 = jnp.zeros((B, RWKV_HEADS, HEAD_DIM, HEAD_DIM), jnp.float32)
    _, o = lax.scan(step, state0, xs)
    o = o.transpose(1, 0, 2, 3)
    mean = jnp.mean(o, axis=-1, keepdims=True)
    var = jnp.mean(jnp.square(o - mean), axis=-1, keepdims=True)
    o = ((o - mean) * lax.rsqrt(var + RWKV_GN_EPS)).reshape(B, S, RWKV_W) * ln_g + ln_b
    bonus = jnp.sum(rh * kh * r_k, axis=-1, keepdims=True) * vh
    return (o + bonus.reshape(B, S, RWKV_W)).astype(z.dtype)


def trailing_mean(z, w):
    S = z.shape[1]
    cs = jnp.pad(jnp.cumsum(z, axis=1), ((0, 0), (1, 0), (0, 0)))
    upper = cs[:, 1:]
    lower = jnp.pad(cs, ((0, 0), (w - 1, 0), (0, 0)))[:, :S]
    count = jnp.minimum(jnp.arange(S) + 1, w).astype(jnp.float32)
    return (upper - lower) / count[None, :, None]


def multiscale_pool(p, pool_w, pool_scale):
    B, S, _ = p.shape
    pf = p.astype(jnp.float32).reshape(B, S, POOL_GROUPS, POOL_GROUP_DIM)
    pooled = jnp.stack([trailing_mean(pf[:, :, g], w) - pf[:, :, g]
                        for g, w in enumerate(POOL_WINDOWS)], axis=2)
    y = jnp.einsum('bsgc,gcd->bsgd', pooled, pool_w.astype(jnp.float32))
    return (y.reshape(B, S, POOL_W) * pool_scale).astype(p.dtype)


def setup_inputs(seed: int = 0) -> dict:
    key = jax.random.key(seed)
    ks = jax.random.split(key, 24)
    nrm = lambda k, shape: jax.random.normal(k, shape, jnp.float32)
    L, D = DEPTH, D_MODEL
    return {
        "x": nrm(ks[0], (BATCH, SEQ, D)),
        "c": nrm(ks[1], (BATCH, D)),
        "ada_w": nrm(ks[2], (L, D, 3 * D)) * D ** -0.5,
        "ada_b": nrm(ks[3], (L, 3 * D)) * 0.02,
        "norm_pre": 1.0 + 0.05 * nrm(ks[4], (L, D)),
        "norm_post": 1.0 + 0.05 * nrm(ks[5], (L, D)),
        "w_in": nrm(ks[6], (L, D, D_IN)) * D ** -0.5,
        "fox_q_gain": 1.0 + 0.05 * nrm(ks[7], (L, HEAD_DIM)),
        "fox_k_gain": 1.0 + 0.05 * nrm(ks[8], (L, HEAD_DIM)),
        "fox_f_bias": 2.0 + 0.5 * nrm(ks[9], (L, FOX_HEADS)),
        "rwkv_mu": jax.random.uniform(ks[10], (L, RWKV_SHIFT_W), jnp.float32),
        "rwkv_w0": 0.5 * nrm(ks[11], (L, RWKV_W)),
        "rwkv_w2": nrm(ks[12], (L, RWKV_DECAY_RANK, RWKV_W)) * 0.1,
        "rwkv_a0": 0.1 * nrm(ks[13], (L, RWKV_W)),
        "rwkv_a2": nrm(ks[14], (L, RWKV_ICL_RANK, RWKV_W)) * 0.1,
        "rwkv_k_k": 0.85 + 0.05 * nrm(ks[15], (L, RWKV_W)),
        "rwkv_k_a": 1.0 + 0.05 * nrm(ks[16], (L, RWKV_W)),
        "rwkv_r_k": 0.1 * nrm(ks[17], (L, RWKV_HEADS, HEAD_DIM)),
        "rwkv_ln_g": 1.0 + 0.05 * nrm(ks[18], (L, RWKV_W)),
        "rwkv_ln_b": 0.02 * nrm(ks[19], (L, RWKV_W)),
        "pool_w": nrm(ks[20], (L, POOL_GROUPS, POOL_GROUP_DIM, POOL_GROUP_DIM)) * POOL_GROUP_DIM ** -0.5,
        "pool_scale": 1.0 + 0.1 * nrm(ks[21], (L, POOL_W)),
        "w_out": nrm(ks[22], (L, D_MIX, D)) * D_MIX ** -0.5,
    }


def reference(x, c, ada_w, ada_b, norm_pre, norm_post, w_in, fox_q_gain, fox_k_gain, fox_f_bias,
              rwkv_mu, rwkv_w0, rwkv_w2, rwkv_a0, rwkv_a2, rwkv_k_k, rwkv_k_a, rwkv_r_k,
              rwkv_ln_g, rwkv_ln_b, pool_w, pool_scale, w_out):
    for l in range(DEPTH):
        mod = jax.nn.silu(c) @ ada_w[l] + ada_b[l]
        shift, scale, gate = jnp.split(mod[:, None, :], 3, axis=-1)
        h = rms_norm(x, norm_pre[l]) * (1.0 + scale) + shift
        u = h @ w_in[l]
        q, k, v, f_logit, z_rwkv, p_pool, g = jnp.split(u, IN_IDX, axis=-1)
        y_fox = fox_attention(q, k, v, f_logit, fox_q_gain[l], fox_k_gain[l], fox_f_bias[l])
        y_rwkv = rwkv7_time_mix(z_rwkv, rwkv_mu[l], rwkv_w0[l], rwkv_w2[l], rwkv_a0[l], rwkv_a2[l],
                                rwkv_k_k[l], rwkv_k_a[l], rwkv_r_k[l], rwkv_ln_g[l], rwkv_ln_b[l])
        y_pool = multiscale_pool(p_pool, pool_w[l], pool_scale[l])
        y = jnp.concatenate([y_fox, y_rwkv, y_pool], axis=-1) * jax.nn.silu(g)
        x = x + gate * rms_norm(y @ w_out[l], norm_post[l])
    return x
```

```python
import functools
import math

import jax
import jax.numpy as jnp
from jax import lax
from jax.experimental import pallas as pl
from jax.experimental.pallas import tpu as pltpu

F32 = jnp.float32
BF16 = jnp.bfloat16

D_MODEL = 1024
HEAD_DIM = 64
FOX_HEADS = 8
FOX_W = FOX_HEADS * HEAD_DIM
RWKV_HEADS = 4
RWKV_W = RWKV_HEADS * HEAD_DIM
POOL_GROUPS = 4
POOL_GROUP_DIM = 64
POOL_W = POOL_GROUPS * POOL_GROUP_DIM
POOL_WINDOWS = (2, 4, 8, 16)
POOL_HALO = 16
D_MIX = FOX_W + RWKV_W + POOL_W
RWKV_DECAY_RANK = 64
RWKV_ICL_RANK = 64
RWKV_SHIFT_W = 3 * RWKV_W + RWKV_DECAY_RANK + RWKV_ICL_RANK
NORM_EPS = 1e-6
RWKV_GN_EPS = 64e-5
DECAY_SCALE = float(math.exp(-0.5))
LOG2E = float(1.0 / math.log(2.0))

LANES = 128
F_PAD = LANES
CHUNK = 64
NEG_BIG = -1e30

VMEM_LIMIT = 56 * 1024 * 1024


def _cparams(sem):
    return pltpu.CompilerParams(dimension_semantics=sem, vmem_limit_bytes=VMEM_LIMIT)


def _dot(a, b):
    return jnp.dot(a, b, preferred_element_type=F32)


def _dot_nt(a, b):
    return lax.dot_general(a, b, (((1,), (1,)), ((), ())), preferred_element_type=F32)


def _dot_tn(a, b):
    return lax.dot_general(a, b, (((0,), (0,)), ((), ())), preferred_element_type=F32)


def _split2(x):
    hi = x.astype(BF16)
    lo = (x - hi.astype(F32)).astype(BF16)
    return hi, lo


def _split3(x):
    hi = x.astype(BF16)
    r = x - hi.astype(F32)
    mid = r.astype(BF16)
    lo = (r - mid.astype(F32)).astype(BF16)
    return hi, mid, lo


def _dot_exact_lhs(l_bf16, x):
    h, m, l = _split3(x)
    return _dot(l_bf16, h) + _dot(l_bf16, m) + _dot(l_bf16, l)


def _dot_exact_rhs(x, r_bf16):
    h, m, l = _split3(x)
    return _dot(h, r_bf16) + _dot(m, r_bf16) + _dot(l, r_bf16)


def _dot3(a, b, dot=_dot):
    ah, al = _split2(a)
    bh, bl = _split2(b)
    return dot(ah, bh) + dot(ah, bl) + dot(al, bh)


def _sigmoid(x):
    return 1.0 / (1.0 + jnp.exp(-x))


def _silu(x):
    return x * _sigmoid(x)


def _mod_kernel(c_ref, w_ref, b_ref, o_ref):
    sc = _silu(c_ref[...])
    o_ref[0] = jnp.dot(sc, w_ref[0], precision=lax.Precision.HIGHEST,
                       preferred_element_type=F32) + b_ref[0]


def _adaln_mod(c, ada_w, ada_b):
    L, D, D3 = ada_w.shape
    B = c.shape[0]
    nj = D3 // D
    return pl.pallas_call(
        _mod_kernel,
        grid=(L, nj),
        in_specs=[
            pl.BlockSpec((B, D), lambda l, j: (0, 0)),
            pl.BlockSpec((1, D, D), lambda l, j: (l, 0, j)),
            pl.BlockSpec((1, 1, D), lambda l, j: (l, 0, j)),
        ],
        out_specs=pl.BlockSpec((1, B, D), lambda l, j: (l, 0, j)),
        out_shape=jax.ShapeDtypeStruct((L, B, D3), F32),
        compiler_params=_cparams(("arbitrary", "arbitrary")),
        name="adaln_mod",
    )(c, ada_w, ada_b.reshape(L, 1, D3))


def _in_kernel(x_ref, mod_ref, npre_ref, wq_ref, wk_ref, wv_ref, wf_ref, wz_ref, wp_ref, wg_ref,
               fb_ref, qg_ref, kg_ref, ltri_ref,
               q_out, k_out, v_out, z_out, p_out, g_out, carry_ref):
    si = pl.program_id(1)

    @pl.when(si == 0)
    def _():
        carry_ref[...] = jnp.zeros_like(carry_ref)

    D = D_MODEL
    x = x_ref[0]
    ts = x.shape[0]
    ms = jnp.mean(x * x, axis=-1, keepdims=True)
    xn = x * lax.rsqrt(ms + NORM_EPS) * npre_ref[...]
    mod = mod_ref[0]
    shift = mod[:, 0:D]
    scale = mod[:, D:2 * D]
    h = (xn * (1.0 + scale) + shift).astype(BF16)

    z_out[0] = _dot(h, wz_ref[...])
    p_out[0] = _dot(h, wp_ref[...])
    g_out[0] = _dot(h, wg_ref[...])

    f = _dot(h, wf_ref[...]) + fb_ref[...]
    logf = (jnp.minimum(f, 0.0) - jnp.log1p(jnp.exp(-jnp.abs(f)))) * LOG2E
    cum = _dot_exact_lhs(ltri_ref[...], logf) + carry_ref[...]
    carry_ref[...] = cum[ts - 1:ts, :]
    c1 = cum.astype(BF16).astype(F32)
    r1 = cum - c1
    c2 = r1.astype(BF16).astype(F32)
    c3 = r1 - c2

    uq = _dot(h, wq_ref[...])
    uk = _dot(h, wk_ref[...])
    uv = _dot(h, wv_ref[...])

    lane = lax.broadcasted_iota(jnp.int32, (ts, LANES), 1)
    lo = lane < HEAD_DIM
    qscale = (HEAD_DIM ** -0.5) * LOG2E

    def head_norm(x2, gain2):
        sq = x2 * x2
        ss_lo = jnp.sum(jnp.where(lo, sq, 0.0), axis=-1, keepdims=True)
        ss_hi = jnp.sum(jnp.where(lo, 0.0, sq), axis=-1, keepdims=True)
        inv = jnp.where(lo, lax.rsqrt(ss_lo * (1.0 / HEAD_DIM) + NORM_EPS),
                        lax.rsqrt(ss_hi * (1.0 / HEAD_DIM) + NORM_EPS))
        return x2 * inv * gain2

    def extras(e0, hcol, for_q):
        a1 = c1[:, hcol:hcol + 1]
        a2 = c2[:, hcol:hcol + 1]
        a3 = c3[:, hcol:hcol + 1]
        one = jnp.ones((ts, 1), F32)
        if for_q:
            vals = (a1, a2, a3, one, one, one)
        else:
            vals = (one, one, one, -a1, -a2, -a3)
        e = jnp.zeros((ts, LANES), F32)
        for i, val in enumerate(vals):
            e = jnp.where(lane == e0 + i, val, e)
        return e

    for j in range(FOX_HEADS // 2):
        sl = slice(LANES * j, LANES * (j + 1))
        qn = head_norm(uq[:, sl], qg_ref[...]) * qscale
        kn = head_norm(uk[:, sl], kg_ref[...])
        v2 = uv[:, sl]
        he, ho = 2 * j, 2 * j + 1
        q_out[0, he] = jnp.where(lo, qn, extras(HEAD_DIM, he, True)).astype(BF16)
        k_out[0, he] = jnp.where(lo, kn, extras(HEAD_DIM, he, False)).astype(BF16)
        v_out[0, he] = jnp.where(lo, v2, jnp.where(lane == HEAD_DIM, 1.0, 0.0)).astype(BF16)
        q_out[0, ho] = jnp.where(lo, extras(0, ho, True), qn).astype(BF16)
        k_out[0, ho] = jnp.where(lo, extras(0, ho, False), kn).astype(BF16)
        v_out[0, ho] = jnp.where(lo, jnp.where(lane == 0, 1.0, 0.0), v2).astype(BF16)


def _in_proj(x, mod_l, npre, wq, wk, wv, wf, wz, wp, wg, fbias, qg2, kg2, ts):
    B, S, D = x.shape
    ns = S // ts
    ltri = jnp.tril(jnp.ones((ts, ts), F32)).astype(BF16)
    full = lambda a: pl.BlockSpec(a.shape, lambda b, s: (0,) * a.ndim)
    hs = jax.ShapeDtypeStruct((B, FOX_HEADS, S, LANES), BF16)
    head_spec = pl.BlockSpec((1, FOX_HEADS, ts, LANES), lambda b, s: (b, 0, s, 0))
    seq_spec = lambda w: pl.BlockSpec((1, ts, w), lambda b, s: (b, s, 0))
    return pl.pallas_call(
        _in_kernel,
        grid=(B, ns),
        in_specs=[
            seq_spec(D),
            pl.BlockSpec((1, 1, 3 * D), lambda b, s: (b, 0, 0)),
            full(npre), full(wq), full(wk), full(wv), full(wf), full(wz), full(wp), full(wg),
            full(fbias), full(qg2), full(kg2), full(ltri),
        ],
        out_specs=[head_spec, head_spec, head_spec,
                   seq_spec(RWKV_SHIFT_W), seq_spec(POOL_W), seq_spec(D_MIX)],
        out_shape=[hs, hs, hs,
                   jax.ShapeDtypeStruct((B, S, RWKV_SHIFT_W), F32),
                   jax.ShapeDtypeStruct((B, S, POOL_W), F32),
                   jax.ShapeDtypeStruct((B, S, D_MIX), F32)],
        scratch_shapes=[pltpu.VMEM((1, LANES), F32)],
        compiler_params=_cparams(("arbitrary", "arbitrary")),
        name="in_proj",
    )(x, mod_l, npre, wq, wk, wv, wf, wz, wp, wg, fbias, qg2, kg2, ltri)


def _attn_kernel(q_ref, k_ref, v_ref, o_ref, *, tq):
    qi = pl.program_id(2)
    row = lax.broadcasted_iota(jnp.int32, (tq, tq), 0)
    col = lax.broadcasted_iota(jnp.int32, (tq, tq), 1)
    causal = col <= row
    lane = lax.broadcasted_iota(jnp.int32, (tq, LANES), 1)
    lo = lane < HEAD_DIM

    accs = []
    for hh in range(2):
        q = q_ref[0, hh]

        def step(kb, vb, m, acc, mask):
            s = _dot_nt(q, kb)
            if mask:
                s = jnp.where(causal, s, NEG_BIG)
            m_new = jnp.maximum(m, jnp.max(s, axis=-1, keepdims=True))
            alpha = jnp.exp2(m - m_new)
            p = jnp.exp2(s - m_new)
            acc = alpha * acc + _dot(p.astype(BF16), vb)
            return m_new, acc

        def body(j, carry):
            m, acc = carry
            off = pl.multiple_of(j * tq, tq)
            kb = k_ref[0, hh, pl.ds(off, tq), :]
            vb = v_ref[0, hh, pl.ds(off, tq), :]
            return step(kb, vb, m, acc, False)

        m0 = jnp.full((tq, 1), NEG_BIG, F32)
        a0 = jnp.zeros((tq, LANES), F32)
        m, acc = lax.fori_loop(0, qi, body, (m0, a0))
        off = pl.multiple_of(qi * tq, tq)
        m, acc = step(k_ref[0, hh, pl.ds(off, tq), :], v_ref[0, hh, pl.ds(off, tq), :], m, acc, True)
        accs.append(acc)

    l_e = accs[0][:, HEAD_DIM:HEAD_DIM + 1]
    l_o = accs[1][:, 0:1]
    o_ref[0] = jnp.where(lo, accs[0] / l_e, accs[1] / l_o)


def _fox_attention(q, k, v, tq):
    B, H, S, _ = q.shape
    nq = S // tq
    return pl.pallas_call(
        functools.partial(_attn_kernel, tq=tq),
        grid=(B, H // 2, nq),
        in_specs=[
            pl.BlockSpec((1, 2, tq, LANES), lambda b, p, i: (b, p, i, 0)),
            pl.BlockSpec((1, 2, S, LANES), lambda b, p, i: (b, p, 0, 0)),
            pl.BlockSpec((1, 2, S, LANES), lambda b, p, i: (b, p, 0, 0)),
        ],
        out_specs=pl.BlockSpec((1, tq, LANES), lambda b, p, i: (b, i, p)),
        out_shape=jax.ShapeDtypeStruct((B, S, FOX_W), F32),
        compiler_params=_cparams(("arbitrary", "arbitrary", "arbitrary")),
        name="fox_attention",
    )(q, k, v)


def _unit_lower_inverse(m_strict):
    n = m_strict.shape[0]
    row = lax.broadcasted_iota(jnp.int32, (n, n), 0)
    col = lax.broadcasted_iota(jnp.int32, (n, n), 1)
    eye = (row == col).astype(F32)
    same16 = (row >> 4) == (col >> 4)
    same32 = (row >> 5) == (col >> 5)
    n1 = jnp.where(same16, m_strict, 0.0)
    n2 = _dot3(n1, n1)
    n4 = _dot3(n2, n2)
    n8 = _dot3(n4, n4)
    t = eye + n1 + n2 + _dot3(n1, n2)
    t = t + _dot3(t, n4)
    t = t + _dot3(t, n8)
    e = jnp.where(jnp.logical_and(same32, jnp.logical_not(same16)), m_strict, 0.0)
    t = t + _dot3(t, _dot3(e, t))
    e = jnp.where(same32, 0.0, m_strict)
    t = t + _dot3(t, _dot3(e, t))
    return t


def _rwkv_kernel(z_ref, mu_ref, w0_ref, a0_ref, kk_ref, ka_ref, rk_ref, lng_ref, lnb_ref,
                 w2a2_ref, seg_ref, ltri_ref, y_ref, state_ref, prev_ref, o_scr):
    si = pl.program_id(1)

    @pl.when(si == 0)
    def _():
        state_ref[...] = jnp.zeros_like(state_ref)
        prev_ref[...] = jnp.zeros_like(prev_ref)

    W = RWKV_W
    N = HEAD_DIM
    z = z_ref[0]
    cb = z.shape[0]
    row = lax.broadcasted_iota(jnp.int32, (cb, 1), 0)
    zprev = jnp.where(row == 0, prev_ref[...], pltpu.roll(z, 1, 0))
    prev_ref[...] = z[cb - 1:cb, :]
    zf = z + (zprev - z) * mu_ref[...]
    r = zf[:, 0:W]
    k = zf[:, W:2 * W]
    v = zf[:, 2 * W:3 * W]
    t6 = zf[:, 3 * W:3 * W + 2 * N]
    lane = lax.broadcasted_iota(jnp.int32, (cb, 2 * N), 1)
    tz = jnp.where(lane < N, jnp.tanh(t6), t6)
    dwa = _dot3(tz, w2a2_ref[...])
    logw = -DECAY_SCALE * _sigmoid(w0_ref[...] + dwa[:, 0:W])
    a = _sigmoid(a0_ref[...] + dwa[:, W:2 * W])

    seg = seg_ref[...]
    kkr = k * kk_ref[...]
    kk = kkr / jnp.maximum(jnp.sqrt(_dot_exact_rhs(kkr * kkr, seg)), 1e-12)
    k2 = k * (1.0 + (a - 1.0) * ka_ref[...])
    bonus = _dot_exact_rhs(r * k2 * rk_ref[...], seg) * v

    logg = _dot_exact_lhs(ltri_ref[...], logw)
    g = jnp.exp(logg)
    gex = jnp.exp(logg - logw)
    gi = jnp.exp(-logg)
    at = -kk * gex
    rt = r * g
    bt = kk * a * gi
    kt = k2 * gi

    r64 = lax.broadcasted_iota(jnp.int32, (CHUNK, CHUNK), 0)
    c64 = lax.broadcasted_iota(jnp.int32, (CHUNK, CHUNK), 1)
    strict = c64 < r64
    incl = c64 <= r64

    nchunk = cb // CHUNK
    pre = []
    for c in range(nchunk):
        rs = slice(CHUNK * c, CHUNK * (c + 1))
        glast = g[CHUNK * (c + 1) - 1:CHUNK * (c + 1), :]
        per_head = []
        for hd in range(RWKV_HEADS):
            ls = slice(N * hd, N * (hd + 1))
            a_h = at[rs, ls]
            r_h = rt[rs, ls]
            b_h = bt[rs, ls]
            k_h = kt[rs, ls]
            v_h = v[rs, ls]
            gl_h = glast[:, ls]
            p_ab = _dot3(a_h, b_h, _dot_nt)
            p_ak = _dot3(a_h, k_h, _dot_nt)
            p_rb = _dot3(r_h, b_h, _dot_nt)
            p_rk = _dot3(r_h, k_h, _dot_nt)
            m_ab = jnp.where(strict, p_ab, 0.0)
            m_ak = jnp.where(strict, p_ak, 0.0)
            m_rb = jnp.where(incl, p_rb, 0.0)
            m_rk = jnp.where(incl, p_rk, 0.0)
            t = _unit_lower_inverse(m_ab)
            w_h = _dot3(t, a_h)
            u0 = _dot3(t, _dot3(m_ak, v_h))
            o0 = _dot3(m_rk, v_h)
            bh = b_h * gl_h
            kh = k_h * gl_h
            d0 = _dot3(v_h, kh, _dot_tn)
            per_head.append((w_h, u0, r_h, m_rb, o0, bh, d0, gl_h))
        pre.append(per_head)

    for c in range(nchunk):
        for hd in range(RWKV_HEADS):
            w_h, u0, r_h, m_rb, o0, bh, d0, gl_h = pre[c][hd]
            s0 = state_ref[hd]
            u = _dot3(w_h, s0, _dot_nt) + u0
            o = _dot3(r_h, s0, _dot_nt) + _dot3(m_rb, u) + o0
            state_ref[hd] = s0 * gl_h + _dot3(u, bh, _dot_tn) + d0
            o_scr[CHUNK * c:CHUNK * (c + 1), N * hd:N * (hd + 1)] = o

    o = o_scr[...]
    mean = _dot_exact_rhs(o, seg) * (1.0 / N)
    d = o - mean
    var = _dot_exact_rhs(d * d, seg) * (1.0 / N)
    y_ref[0] = d * lax.rsqrt(var + RWKV_GN_EPS) * lng_ref[...] + lnb_ref[...] + bonus


def _rwkv(z, mu, w0, a0, k_k, k_a, r_k, ln_g, ln_b, w2a2, cb):
    B, S, _ = z.shape
    ns = S // cb
    seg = jnp.kron(jnp.eye(RWKV_HEADS, dtype=F32), jnp.ones((HEAD_DIM, HEAD_DIM), F32)).astype(BF16)
    ltri = jnp.kron(jnp.eye(cb // CHUNK, dtype=F32), jnp.tril(jnp.ones((CHUNK, CHUNK), F32))).astype(BF16)
    full = lambda a: pl.BlockSpec(a.shape, lambda b, s: (0,) * a.ndim)
    args = (mu, w0, a0, k_k, k_a, r_k, ln_g, ln_b, w2a2, seg, ltri)
    return pl.pallas_call(
        _rwkv_kernel,
        grid=(B, ns),
        in_specs=[pl.BlockSpec((1, cb, RWKV_SHIFT_W), lambda b, s: (b, s, 0))] + [full(a) for a in args],
        out_specs=pl.BlockSpec((1, cb, RWKV_W), lambda b, s: (b, s, 0)),
        out_shape=jax.ShapeDtypeStruct((B, S, RWKV_W), F32),
        scratch_shapes=[pltpu.VMEM((RWKV_HEADS, HEAD_DIM, HEAD_DIM), F32),
                        pltpu.VMEM((1, RWKV_SHIFT_W), F32),
                        pltpu.VMEM((cb, RWKV_W), F32)],
        compiler_params=_cparams(("arbitrary", "arbitrary")),
        name="rwkv7",
    )(z, *args)


def _out_kernel(x_ref, mod_ref, yf_ref, yr_ref, p_ref, g_ref, pw_ref, ps_ref, wout_ref, npost_ref,
                o_ref, halo_ref):
    si = pl.program_id(1)

    @pl.when(si == 0)
    def _():
        halo_ref[...] = jnp.zeros_like(halo_ref)

    D = D_MODEL
    p = p_ref[0]
    ts = p.shape[0]
    e = jnp.concatenate([halo_ref[...], p], axis=0)
    halo_ref[...] = p[ts - POOL_HALO:ts, :]
    sums = []
    s_run = e
    shift = 1
    for _w in POOL_WINDOWS:
        s_run = s_run + pltpu.roll(s_run, shift, 0)
        sums.append(s_run[POOL_HALO:, :])
        shift *= 2
    pos = si * ts + lax.broadcasted_iota(jnp.int32, (ts, 1), 0) + 1
    lane = lax.broadcasted_iota(jnp.int32, (ts, POOL_W), 1)
    pooled = jnp.zeros((ts, POOL_W), F32)
    for gi, w in enumerate(POOL_WINDOWS):
        cnt = jnp.minimum(pos, w).astype(F32)
        grp = (lane >= POOL_GROUP_DIM * gi) & (lane < POOL_GROUP_DIM * (gi + 1))
        pooled = jnp.where(grp, sums[gi] / cnt, pooled)
    pooled = pooled - p
    y_pool = _dot(pooled.astype(BF16), pw_ref[...]) * ps_ref[...]

    y = jnp.concatenate([yf_ref[0], yr_ref[0], y_pool], axis=-1) * _silu(g_ref[0])
    t = _dot(y.astype(BF16), wout_ref[...])
    ms = jnp.mean(t * t, axis=-1, keepdims=True)
    tn = t * lax.rsqrt(ms + NORM_EPS) * npost_ref[...]
    gate = mod_ref[0][:, 2 * D:3 * D]
    o_ref[0] = x_ref[0] + gate * tn


def _out_proj(x, mod_l, y_fox, y_rwkv, p, g, pw_bd, pool_scale, w_out, npost, ts):
    B, S, D = x.shape
    ns = S // ts
    full = lambda a: pl.BlockSpec(a.shape, lambda b, s: (0,) * a.ndim)
    seq_spec = lambda w: pl.BlockSpec((1, ts, w), lambda b, s: (b, s, 0))
    return pl.pallas_call(
        _out_kernel,
        grid=(B, ns),
        in_specs=[seq_spec(D), pl.BlockSpec((1, 1, 3 * D), lambda b, s: (b, 0, 0)),
                  seq_spec(FOX_W), seq_spec(RWKV_W), seq_spec(POOL_W), seq_spec(D_MIX),
                  full(pw_bd), full(pool_scale), full(w_out), full(npost)],
        out_specs=seq_spec(D),
        out_shape=jax.ShapeDtypeStruct((B, S, D), F32),
        scratch_shapes=[pltpu.VMEM((POOL_HALO, POOL_W), F32)],
        compiler_params=_cparams(("arbitrary", "arbitrary")),
        name="out_proj",
    )(x, mod_l, y_fox, y_rwkv, p, g, pw_bd, pool_scale, w_out, npost)


def _block_diag(blocks):
    n = blocks.shape[0]
    d = blocks.shape[1]
    out = jnp.zeros((n * d, n * d), blocks.dtype)
    for i in range(n):
        out = out.at[i * d:(i + 1) * d, i * d:(i + 1) * d].set(blocks[i])
    return out


def kernel(x, c, ada_w, ada_b, norm_pre, norm_post, w_in, fox_q_gain, fox_k_gain, fox_f_bias, rwkv_mu, rwkv_w0, rwkv_w2, rwkv_a0, rwkv_a2, rwkv_k_k, rwkv_k_a, rwkv_r_k, rwkv_ln_g, rwkv_ln_b, pool_w, pool_scale, w_out):
    B, S, D = x.shape
    L = ada_w.shape[0]
    ts = min(512, S)
    tq = min(512, S)
    cb = min(128, S)

    mod = _adaln_mod(c, ada_w, ada_b)
    o_q, o_k, o_v, o_f, o_z, o_p = (0, FOX_W, 2 * FOX_W, 3 * FOX_W, 3 * FOX_W + FOX_HEADS,
                                    3 * FOX_W + FOX_HEADS + RWKV_SHIFT_W)
    o_g = o_p + POOL_W
    row = lambda a: a.reshape(1, -1)

    for l in range(L):
        w = w_in[l]
        wq = w[:, o_q:o_k].astype(BF16)
        wk = w[:, o_k:o_v].astype(BF16)
        wv = w[:, o_v:o_f].astype(BF16)
        wf = jnp.pad(w[:, o_f:o_z], ((0, 0), (0, F_PAD - FOX_HEADS))).astype(BF16)
        wz = w[:, o_z:o_p].astype(BF16)
        wp = w[:, o_p:o_g].astype(BF16)
        wg = w[:, o_g:].astype(BF16)
        fbias = jnp.pad(fox_f_bias[l], (0, F_PAD - FOX_HEADS)).reshape(1, F_PAD)
        qg2 = jnp.tile(fox_q_gain[l], 2).reshape(1, LANES)
        kg2 = jnp.tile(fox_k_gain[l], 2).reshape(1, LANES)
        mod_l = mod[l].reshape(B, 1, 3 * D)

        q_aug, k_aug, v_aug, z, p, g = _in_proj(x, mod_l, row(norm_pre[l]), wq, wk, wv, wf, wz, wp, wg,
                                                fbias, qg2, kg2, ts)
        y_fox = _fox_attention(q_aug, k_aug, v_aug, tq)

        w2a2 = jnp.zeros((2 * HEAD_DIM, 2 * RWKV_W), F32)
        w2a2 = w2a2.at[:RWKV_DECAY_RANK, :RWKV_W].set(rwkv_w2[l])
        w2a2 = w2a2.at[RWKV_DECAY_RANK:, RWKV_W:].set(rwkv_a2[l])
        y_rwkv = _rwkv(z, row(rwkv_mu[l]), row(rwkv_w0[l]), row(rwkv_a0[l]), row(rwkv_k_k[l]),
                       row(rwkv_k_a[l]), row(rwkv_r_k[l]), row(rwkv_ln_g[l]), row(rwkv_ln_b[l]),
                       w2a2, cb)

        pw_bd = _block_diag(pool_w[l]).astype(BF16)
        x = _out_proj(x, mod_l, y_fox, y_rwkv, p, g, pw_bd, row(pool_scale[l]),
                      w_out[l].astype(BF16), row(norm_post[l]), ts)
    return x
```

```python
import functools
import math

import jax
import jax.numpy as jnp
from jax import lax
from jax.experimental import pallas as pl
from jax.experimental.pallas import tpu as pltpu

F32 = jnp.float32
BF16 = jnp.bfloat16

D_MODEL = 1024
HEAD_DIM = 64
FOX_HEADS = 8
FOX_W = FOX_HEADS * HEAD_DIM
RWKV_HEADS = 4
RWKV_W = RWKV_HEADS * HEAD_DIM
POOL_GROUPS = 4
POOL_GROUP_DIM = 64
POOL_W = POOL_GROUPS * POOL_GROUP_DIM
POOL_WINDOWS = (2, 4, 8, 16)
POOL_HALO = 16
D_MIX = FOX_W + RWKV_W + POOL_W
RWKV_DECAY_RANK = 64
RWKV_ICL_RANK = 64
RWKV_SHIFT_W = 3 * RWKV_W + RWKV_DECAY_RANK + RWKV_ICL_RANK
NORM_EPS = 1e-6
RWKV_GN_EPS = 64e-5
DECAY_SCALE = float(math.exp(-0.5))
LOG2E = float(1.0 / math.log(2.0))

LANES = 128
F_PAD = LANES
CHUNK = 64
NEG_BIG = -1e30

VMEM_LIMIT = 56 * 1024 * 1024


def _cparams(sem):
    return pltpu.CompilerParams(dimension_semantics=sem, vmem_limit_bytes=VMEM_LIMIT)


def _dot(a, b):
    return jnp.dot(a, b, preferred_element_type=F32)


def _dot_nt(a, b):
    return lax.dot_general(a, b, (((1,), (1,)), ((), ())), preferred_element_type=F32)


def _dot_tn(a, b):
    return lax.dot_general(a, b, (((0,), (0,)), ((), ())), preferred_element_type=F32)


def _split2(x):
    hi = x.astype(BF16)
    lo = (x - hi.astype(F32)).astype(BF16)
    return hi, lo


def _split3(x):
    hi = x.astype(BF16)
    r = x - hi.astype(F32)
    mid = r.astype(BF16)
    lo = (r - mid.astype(F32)).astype(BF16)
    return hi, mid, lo


def _dot_exact_lhs(l_bf16, x):
    h, m, l = _split3(x)
    return _dot(l_bf16, h) + _dot(l_bf16, m) + _dot(l_bf16, l)


def _dot_exact_rhs(x, r_bf16):
    h, m, l = _split3(x)
    return _dot(h, r_bf16) + _dot(m, r_bf16) + _dot(l, r_bf16)


def _dot3(a, b, dot=_dot):
    ah, al = _split2(a)
    bh, bl = _split2(b)
    return dot(ah, bh) + dot(ah, bl) + dot(al, bh)


def _sigmoid(x):
    return 1.0 / (1.0 + jnp.exp(-x))


def _silu(x):
    return x * _sigmoid(x)


def _mod_kernel(c_ref, w_ref, b_ref, o_ref):
    sc = _silu(c_ref[...])
    o_ref[0] = jnp.dot(sc, w_ref[0], precision=lax.Precision.HIGHEST,
                       preferred_element_type=F32) + b_ref[0]


def _adaln_mod(c, ada_w, ada_b):
    L, D, D3 = ada_w.shape
    B = c.shape[0]
    nj = D3 // D
    return pl.pallas_call(
        _mod_kernel,
        grid=(L, nj),
        in_specs=[
            pl.BlockSpec((B, D), lambda l, j: (0, 0)),
            pl.BlockSpec((1, D, D), lambda l, j: (l, 0, j)),
            pl.BlockSpec((1, 1, D), lambda l, j: (l, 0, j)),
        ],
        out_specs=pl.BlockSpec((1, B, D), lambda l, j: (l, 0, j)),
        out_shape=jax.ShapeDtypeStruct((L, B, D3), F32),
        compiler_params=_cparams(("arbitrary", "arbitrary")),
        name="adaln_mod",
    )(c, ada_w, ada_b.reshape(L, 1, D3))


def _in_kernel(x_ref, mod_ref, npre_ref, wq_ref, wk_ref, wv_ref, wf_ref, wz_ref, wp_ref, wg_ref,
               fb_ref, qg_ref, kg_ref, ltri_ref,
               q_out, k_out, v_out, z_out, p_out, g_out, carry_ref):
    si = pl.program_id(1)

    @pl.when(si == 0)
    def _():
        carry_ref[...] = jnp.zeros_like(carry_ref)

    D = D_MODEL
    x = x_ref[0]
    ts = x.shape[0]
    ms = jnp.mean(x * x, axis=-1, keepdims=True)
    xn = x * lax.rsqrt(ms + NORM_EPS) * npre_ref[...]
    mod = mod_ref[0]
    shift = mod[:, 0:D]
    scale = mod[:, D:2 * D]
    h = (xn * (1.0 + scale) + shift).astype(BF16)

    z_out[0] = _dot(h, wz_ref[...])
    p_out[0] = _dot(h, wp_ref[...])
    g_out[0] = _dot(h, wg_ref[...])

    f = _dot(h, wf_ref[...]) + fb_ref[...]
    logf = (jnp.minimum(f, 0.0) - jnp.log1p(jnp.exp(-jnp.abs(f)))) * LOG2E
    cum = _dot_exact_lhs(ltri_ref[...], logf) + carry_ref[...]
    carry_ref[...] = cum[ts - 1:ts, :]
    c1 = cum.astype(BF16).astype(F32)
    r1 = cum - c1
    c2 = r1.astype(BF16).astype(F32)
    c3 = r1 - c2

    uq = _dot(h, wq_ref[...])
    uk = _dot(h, wk_ref[...])
    uv = _dot(h, wv_ref[...])

    lane = lax.broadcasted_iota(jnp.int32, (ts, LANES), 1)
    lo = lane < HEAD_DIM
    qscale = (HEAD_DIM ** -0.5) * LOG2E

    def head_norm(x2, gain2):
        sq = x2 * x2
        ss_lo = jnp.sum(jnp.where(lo, sq, 0.0), axis=-1, keepdims=True)
        ss_hi = jnp.sum(jnp.where(lo, 0.0, sq), axis=-1, keepdims=True)
        inv = jnp.where(lo, lax.rsqrt(ss_lo * (1.0 / HEAD_DIM) + NORM_EPS),
                        lax.rsqrt(ss_hi * (1.0 / HEAD_DIM) + NORM_EPS))
        return x2 * inv * gain2

    def extras(e0, hcol, for_q):
        a1 = c1[:, hcol:hcol + 1]
        a2 = c2[:, hcol:hcol + 1]
        a3 = c3[:, hcol:hcol + 1]
        one = jnp.ones((ts, 1), F32)
        if for_q:
            vals = (a1, a2, a3, one, one, one)
        else:
            vals = (one, one, one, -a1, -a2, -a3)
        e = jnp.zeros((ts, LANES), F32)
        for i, val in enumerate(vals):
            e = jnp.where(lane == e0 + i, val, e)
        return e

    for j in range(FOX_HEADS // 2):
        sl = slice(LANES * j, LANES * (j + 1))
        qn = head_norm(uq[:, sl], qg_ref[...]) * qscale
        kn = head_norm(uk[:, sl], kg_ref[...])
        v2 = uv[:, sl]
        he, ho = 2 * j, 2 * j + 1
        q_out[0, he] = jnp.where(lo, qn, extras(HEAD_DIM, he, True)).astype(BF16)
        k_out[0, he] = jnp.where(lo, kn, extras(HEAD_DIM, he, False)).astype(BF16)
        v_out[0, he] = jnp.where(lo, v2, jnp.where(lane == HEAD_DIM, 1.0, 0.0)).astype(BF16)
        q_out[0, ho] = jnp.where(lo, extras(0, ho, True), qn).astype(BF16)
        k_out[0, ho] = jnp.where(lo, extras(0, ho, False), kn).astype(BF16)
        v_out[0, ho] = jnp.where(lo, jnp.where(lane == 0, 1.0, 0.0), v2).astype(BF16)


def _in_proj(x, mod_l, npre, wq, wk, wv, wf, wz, wp, wg, fbias, qg2, kg2, ts):
    B, S, D = x.shape
    ns = S // ts
    ltri = jnp.tril(jnp.ones((ts, ts), F32)).astype(BF16)
    full = lambda a: pl.BlockSpec(a.shape, lambda b, s: (0,) * a.ndim)
    hs = jax.ShapeDtypeStruct((B, FOX_HEADS, S, LANES), BF16)
    head_spec = pl.BlockSpec((1, FOX_HEADS, ts, LANES), lambda b, s: (b, 0, s, 0))
    seq_spec = lambda w: pl.BlockSpec((1, ts, w), lambda b, s: (b, s, 0))
    return pl.pallas_call(
        _in_kernel,
        grid=(B, ns),
        in_specs=[
            seq_spec(D),
            pl.BlockSpec((1, 1, 3 * D), lambda b, s: (b, 0, 0)),
            full(npre), full(wq), full(wk), full(wv), full(wf), full(wz), full(wp), full(wg),
            full(fbias), full(qg2), full(kg2), full(ltri),
        ],
        out_specs=[head_spec, head_spec, head_spec,
                   seq_spec(RWKV_SHIFT_W), seq_spec(POOL_W), seq_spec(D_MIX)],
        out_shape=[hs, hs, hs,
                   jax.ShapeDtypeStruct((B, S, RWKV_SHIFT_W), F32),
                   jax.ShapeDtypeStruct((B, S, POOL_W), F32),
                   jax.ShapeDtypeStruct((B, S, D_MIX), F32)],
        scratch_shapes=[pltpu.VMEM((1, LANES), F32)],
        compiler_params=_cparams(("arbitrary", "arbitrary")),
        name="in_proj",
    )(x, mod_l, npre, wq, wk, wv, wf, wz, wp, wg, fbias, qg2, kg2, ltri)


def _attn_kernel(q_ref, k_ref, v_ref, o_ref, *, tq):
    qi = pl.program_id(2)
    row = lax.broadcasted_iota(jnp.int32, (tq, tq), 0)
    col = lax.broadcasted_iota(jnp.int32, (tq, tq), 1)
    causal = col <= row
    lane = lax.broadcasted_iota(jnp.int32, (tq, LANES), 1)
    lo = lane < HEAD_DIM

    accs = []
    for hh in range(2):
        q = q_ref[0, hh]

        def step(kb, vb, m, acc, mask):
            s = _dot_nt(q, kb)
            if mask:
                s = jnp.where(causal, s, NEG_BIG)
            m_new = jnp.maximum(m, jnp.max(s, axis=-1, keepdims=True))
            alpha = jnp.exp2(m - m_new)
            p = jnp.exp2(s - m_new)
            acc = alpha * acc + _dot(p.astype(BF16), vb)
            return m_new, acc

        def body(j, carry):
            m, acc = carry
            off = pl.multiple_of(j * tq, tq)
            kb = k_ref[0, hh, pl.ds(off, tq), :]
            vb = v_ref[0, hh, pl.ds(off, tq), :]
            return step(kb, vb, m, acc, False)

        m0 = jnp.full((tq, 1), NEG_BIG, F32)
        a0 = jnp.zeros((tq, LANES), F32)
        m, acc = lax.fori_loop(0, qi, body, (m0, a0))
        off = pl.multiple_of(qi * tq, tq)
        m, acc = step(k_ref[0, hh, pl.ds(off, tq), :], v_ref[0, hh, pl.ds(off, tq), :], m, acc, True)
        accs.append(acc)

    l_e = accs[0][:, HEAD_DIM:HEAD_DIM + 1]
    l_o = accs[1][:, 0:1]
    o_ref[0] = jnp.where(lo, accs[0] / l_e, accs[1] / l_o)


def _fox_attention(q, k, v, tq):
    B, H, S, _ = q.shape
    nq = S // tq
    return pl.pallas_call(
        functools.partial(_attn_kernel, tq=tq),
        grid=(B, H // 2, nq),
        in_specs=[
            pl.BlockSpec((1, 2, tq, LANES), lambda b, p, i: (b, p, i, 0)),
            pl.BlockSpec((1, 2, S, LANES), lambda b, p, i: (b, p, 0, 0)),
            pl.BlockSpec((1, 2, S, LANES), lambda b, p, i: (b, p, 0, 0)),
        ],
        out_specs=pl.BlockSpec((1, tq, LANES), lambda b, p, i: (b, i, p)),
        out_shape=jax.ShapeDtypeStruct((B, S, FOX_W), F32),
        compiler_params=_cparams(("arbitrary", "arbitrary", "arbitrary")),
        name="fox_attention",
    )(q, k, v)


def _mm(a, b):
    return _dot(a.astype(BF16), b.astype(BF16))


def _mm_nt(a, b):
    return _dot_nt(a.astype(BF16), b.astype(BF16))


def _mm_tn(a, b):
    return _dot_tn(a.astype(BF16), b.astype(BF16))


def _unit_lower_inverses(ms, row, col):
    eye = (row == col).astype(F32)
    same = lambda sh: (row >> sh) == (col >> sh)
    ts = [eye + jnp.where(same(1), m, 0.0) for m in ms]
    for sh in range(1, 6):
        band = jnp.logical_and(same(sh + 1), jnp.logical_not(same(sh)))
        ets = [_mm(jnp.where(band, m, 0.0), t) for m, t in zip(ms, ts)]
        ts = [t + _mm(t, et) for t, et in zip(ts, ets)]
    return ts


def _rwkv_kernel(z_ref, mu_ref, w0_ref, a0_ref, kk_ref, ka_ref, rk_ref, lng_ref, lnb_ref,
                 w2a2_ref, seg_ref, ltri_ref, y_ref, state_ref, prev_ref):
    si = pl.program_id(1)

    @pl.when(si == 0)
    def _():
        state_ref[...] = jnp.zeros_like(state_ref)
        prev_ref[...] = jnp.zeros_like(prev_ref)

    W = RWKV_W
    N = HEAD_DIM
    z = z_ref[0]
    cb = z.shape[0]
    row = lax.broadcasted_iota(jnp.int32, (cb, 1), 0)
    zprev = jnp.where(row == 0, prev_ref[...], pltpu.roll(z, 1, 0))
    prev_ref[...] = z[cb - 1:cb, :]
    zf = z + (zprev - z) * mu_ref[...]
    r = zf[:, 0:W]
    k = zf[:, W:2 * W]
    v = zf[:, 2 * W:3 * W]
    t6 = zf[:, 3 * W:3 * W + 2 * N]
    lane = lax.broadcasted_iota(jnp.int32, (cb, 2 * N), 1)
    tz = jnp.where(lane < N, jnp.tanh(t6), t6)
    dwa = _dot3(tz, w2a2_ref[...])
    logw = -DECAY_SCALE * _sigmoid(w0_ref[...] + dwa[:, 0:W])
    a = _sigmoid(a0_ref[...] + dwa[:, W:2 * W])

    seg = seg_ref[...]
    kkr = k * kk_ref[...]
    kk = kkr / jnp.maximum(jnp.sqrt(_dot_exact_rhs(kkr * kkr, seg)), 1e-12)
    k2 = k * (1.0 + (a - 1.0) * ka_ref[...])
    bonus = _dot_exact_rhs(r * k2 * rk_ref[...], seg) * v

    logg = _dot_exact_lhs(ltri_ref[...], logw)
    g = jnp.exp(logg)
    gex = jnp.exp(logg - logw)
    gi = jnp.exp(-logg)
    at = -kk * gex
    rt = r * g
    bt = kk * a * gi
    kt = k2 * gi

    rw = lax.broadcasted_iota(jnp.int32, (W, W), 0)
    cw = lax.broadcasted_iota(jnp.int32, (W, W), 1)
    head_blk = (rw >> 6) == (cw >> 6)
    tt = rw & (CHUNK - 1)
    ss = cw & (CHUNK - 1)
    strict = jnp.logical_and(head_blk, ss < tt)
    incl = jnp.logical_and(head_blk, ss <= tt)

    def tall(zc):
        return jnp.where(head_blk, jnp.concatenate([zc] * RWKV_HEADS, axis=0), 0.0).astype(BF16)

    def fold(y):
        return y[0:CHUNK] + y[CHUNK:2 * CHUNK] + y[2 * CHUNK:3 * CHUNK] + y[3 * CHUNK:4 * CHUNK]

    nchunk = cb // CHUNK
    chunks = range(nchunk)
    rows = [slice(CHUNK * c, CHUNK * (c + 1)) for c in chunks]
    glast = [g[CHUNK * (c + 1) - 1:CHUNK * (c + 1), :] for c in chunks]
    t_a = [tall(at[rs]) for rs in rows]
    t_r = [tall(rt[rs]) for rs in rows]
    t_b = [tall(bt[rs]) for rs in rows]
    t_k = [tall(kt[rs]) for rs in rows]
    t_v = [tall(v[rs]) for rs in rows]
    pp = [_dot_nt(jnp.concatenate([t_a[c], t_r[c]], axis=0), jnp.concatenate([t_b[c], t_k[c]], axis=0))
          for c in chunks]
    m_ab = [jnp.where(strict, p[0:W, 0:W], 0.0) for p in pp]
    m_ak = [jnp.where(strict, p[0:W, W:2 * W], 0.0) for p in pp]
    m_rb = [jnp.where(incl, p[W:2 * W, 0:W], 0.0).astype(BF16) for p in pp]
    m_rk = [jnp.where(incl, p[W:2 * W, W:2 * W], 0.0) for p in pp]
    xo = [_mm(jnp.concatenate([m_ak[c], m_rk[c]], axis=0), t_v[c]) for c in chunks]
    d0 = [jnp.where(head_blk, _mm_tn(v[rows[c]], kt[rows[c]] * glast[c]), 0.0) for c in chunks]
    bh = [(bt[rows[c]] * glast[c]).astype(BF16) for c in chunks]
    tinv = _unit_lower_inverses(m_ab, rw, cw)
    wu = [_mm(tinv[c], jnp.concatenate([t_a[c], xo[c][0:W].astype(BF16)], axis=1)) for c in chunks]
    w_all = [fold(x[:, 0:W]).astype(BF16) for x in wu]
    u0_all = [fold(x[:, W:2 * W]) for x in wu]
    o0_all = [fold(x[W:2 * W]) for x in xo]
    r_all = [rt[rs].astype(BF16) for rs in rows]

    outs = []
    s_bd = state_ref[...]
    for c in chunks:
        y = _dot_nt(jnp.concatenate([w_all[c], r_all[c]], axis=0), s_bd.astype(BF16))
        u = y[0:CHUNK] + u0_all[c]
        outs.append(y[CHUNK:2 * CHUNK] + fold(_dot(m_rb[c], tall(u))) + o0_all[c])
        s_bd = s_bd * glast[c] + jnp.where(head_blk, _dot_tn(u.astype(BF16), bh[c]), 0.0) + d0[c]
    state_ref[...] = s_bd

    o = jnp.concatenate(outs, axis=0)
    mean = _dot_exact_rhs(o, seg) * (1.0 / N)
    d = o - mean
    var = _dot_exact_rhs(d * d, seg) * (1.0 / N)
    y_ref[0] = d * lax.rsqrt(var + RWKV_GN_EPS) * lng_ref[...] + lnb_ref[...] + bonus


def _rwkv(z, mu, w0, a0, k_k, k_a, r_k, ln_g, ln_b, w2a2, cb):
    B, S, _ = z.shape
    ns = S // cb
    seg = jnp.kron(jnp.eye(RWKV_HEADS, dtype=F32), jnp.ones((HEAD_DIM, HEAD_DIM), F32)).astype(BF16)
    ltri = jnp.kron(jnp.eye(cb // CHUNK, dtype=F32), jnp.tril(jnp.ones((CHUNK, CHUNK), F32))).astype(BF16)
    full = lambda a: pl.BlockSpec(a.shape, lambda b, s: (0,) * a.ndim)
    args = (mu, w0, a0, k_k, k_a, r_k, ln_g, ln_b, w2a2, seg, ltri)
    return pl.pallas_call(
        _rwkv_kernel,
        grid=(B, ns),
        in_specs=[pl.BlockSpec((1, cb, RWKV_SHIFT_W), lambda b, s: (b, s, 0))] + [full(a) for a in args],
        out_specs=pl.BlockSpec((1, cb, RWKV_W), lambda b, s: (b, s, 0)),
        out_shape=jax.ShapeDtypeStruct((B, S, RWKV_W), F32),
        scratch_shapes=[pltpu.VMEM((RWKV_W, RWKV_W), F32),
                        pltpu.VMEM((1, RWKV_SHIFT_W), F32)],
        compiler_params=_cparams(("arbitrary", "arbitrary")),
        name="rwkv7",
    )(z, *args)


def _out_kernel(x_ref, mod_ref, yf_ref, yr_ref, p_ref, g_ref, pw_ref, ps_ref, wout_ref, npost_ref,
                o_ref, halo_ref):
    si = pl.program_id(1)

    @pl.when(si == 0)
    def _():
        halo_ref[...] = jnp.zeros_like(halo_ref)

    D = D_MODEL
    p = p_ref[0]
    ts = p.shape[0]
    e = jnp.concatenate([halo_ref[...], p], axis=0)
    halo_ref[...] = p[ts - POOL_HALO:ts, :]
    sums = []
    s_run = e
    shift = 1
    for _w in POOL_WINDOWS:
        s_run = s_run + pltpu.roll(s_run, shift, 0)
        sums.append(s_run[POOL_HALO:, :])
        shift *= 2
    pos = si * ts + lax.broadcasted_iota(jnp.int32, (ts, 1), 0) + 1
    lane = lax.broadcasted_iota(jnp.int32, (ts, POOL_W), 1)
    pooled = jnp.zeros((ts, POOL_W), F32)
    for gi, w in enumerate(POOL_WINDOWS):
        cnt = jnp.minimum(pos, w).astype(F32)
        grp = (lane >= POOL_GROUP_DIM * gi) & (lane < POOL_GROUP_DIM * (gi + 1))
        pooled = jnp.where(grp, sums[gi] / cnt, pooled)
    pooled = pooled - p
    y_pool = _dot(pooled.astype(BF16), pw_ref[...]) * ps_ref[...]

    y = jnp.concatenate([yf_ref[0], yr_ref[0], y_pool], axis=-1) * _silu(g_ref[0])
    t = _dot(y.astype(BF16), wout_ref[...])
    ms = jnp.mean(t * t, axis=-1, keepdims=True)
    tn = t * lax.rsqrt(ms + NORM_EPS) * npost_ref[...]
    gate = mod_ref[0][:, 2 * D:3 * D]
    o_ref[0] = x_ref[0] + gate * tn


def _out_proj(x, mod_l, y_fox, y_rwkv, p, g, pw_bd, pool_scale, w_out, npost, ts):
    B, S, D = x.shape
    ns = S // ts
    full = lambda a: pl.BlockSpec(a.shape, lambda b, s: (0,) * a.ndim)
    seq_spec = lambda w: pl.BlockSpec((1, ts, w), lambda b, s: (b, s, 0))
    return pl.pallas_call(
        _out_kernel,
        grid=(B, ns),
        in_specs=[seq_spec(D), pl.BlockSpec((1, 1, 3 * D), lambda b, s: (b, 0, 0)),
                  seq_spec(FOX_W), seq_spec(RWKV_W), seq_spec(POOL_W), seq_spec(D_MIX),
                  full(pw_bd), full(pool_scale), full(w_out), full(npost)],
        out_specs=seq_spec(D),
        out_shape=jax.ShapeDtypeStruct((B, S, D), F32),
        scratch_shapes=[pltpu.VMEM((POOL_HALO, POOL_W), F32)],
        compiler_params=_cparams(("arbitrary", "arbitrary")),
        name="out_proj",
    )(x, mod_l, y_fox, y_rwkv, p, g, pw_bd, pool_scale, w_out, npost)


def _block_diag(blocks):
    n = blocks.shape[0]
    d = blocks.shape[1]
    out = jnp.zeros((n * d, n * d), blocks.dtype)
    for i in range(n):
        out = out.at[i * d:(i + 1) * d, i * d:(i + 1) * d].set(blocks[i])
    return out


def kernel(x, c, ada_w, ada_b, norm_pre, norm_post, w_in, fox_q_gain, fox_k_gain, fox_f_bias, rwkv_mu, rwkv_w0, rwkv_w2, rwkv_a0, rwkv_a2, rwkv_k_k, rwkv_k_a, rwkv_r_k, rwkv_ln_g, rwkv_ln_b, pool_w, pool_scale, w_out):
    B, S, D = x.shape
    L = ada_w.shape[0]
    ts = min(512, S)
    tq = min(512, S)
    cb = min(256, S)

    mod = _adaln_mod(c, ada_w, ada_b)
    o_q, o_k, o_v, o_f, o_z, o_p = (0, FOX_W, 2 * FOX_W, 3 * FOX_W, 3 * FOX_W + FOX_HEADS,
                                    3 * FOX_W + FOX_HEADS + RWKV_SHIFT_W)
    o_g = o_p + POOL_W
    row = lambda a: a.reshape(1, -1)

    for l in range(L):
        w = w_in[l]
        wq = w[:, o_q:o_k].astype(BF16)
        wk = w[:, o_k:o_v].astype(BF16)
        wv = w[:, o_v:o_f].astype(BF16)
        wf = jnp.pad(w[:, o_f:o_z], ((0, 0), (0, F_PAD - FOX_HEADS))).astype(BF16)
        wz = w[:, o_z:o_p].astype(BF16)
        wp = w[:, o_p:o_g].astype(BF16)
        wg = w[:, o_g:].astype(BF16)
        fbias = jnp.pad(fox_f_bias[l], (0, F_PAD - FOX_HEADS)).reshape(1, F_PAD)
        qg2 = jnp.tile(fox_q_gain[l], 2).reshape(1, LANES)
        kg2 = jnp.tile(fox_k_gain[l], 2).reshape(1, LANES)
        mod_l = mod[l].reshape(B, 1, 3 * D)

        q_aug, k_aug, v_aug, z, p, g = _in_proj(x, mod_l, row(norm_pre[l]), wq, wk, wv, wf, wz, wp, wg,
                                                fbias, qg2, kg2, ts)
        y_fox = _fox_attention(q_aug, k_aug, v_aug, tq)

        w2a2 = jnp.zeros((2 * HEAD_DIM, 2 * RWKV_W), F32)
        w2a2 = w2a2.at[:RWKV_DECAY_RANK, :RWKV_W].set(rwkv_w2[l])
        w2a2 = w2a2.at[RWKV_DECAY_RANK:, RWKV_W:].set(rwkv_a2[l])
        y_rwkv = _rwkv(z, row(rwkv_mu[l]), row(rwkv_w0[l]), row(rwkv_a0[l]), row(rwkv_k_k[l]),
                       row(rwkv_k_a[l]), row(rwkv_r_k[l]), row(rwkv_ln_g[l]), row(rwkv_ln_b[l]),
                       w2a2, cb)

        pw_bd = _block_diag(pool_w[l]).astype(BF16)
        x = _out_proj(x, mod_l, y_fox, y_rwkv, p, g, pw_bd, row(pool_scale[l]),
                      w_out[l].astype(BF16), row(norm_post[l]), ts)
    return x
```

```python
import functools
import math

import jax
import jax.numpy as jnp
from jax import lax
from jax.experimental import pallas as pl
from jax.experimental.pallas import tpu as pltpu

F32 = jnp.float32
BF16 = jnp.bfloat16

D_MODEL = 1024
HEAD_DIM = 64
FOX_HEADS = 8
FOX_W = FOX_HEADS * HEAD_DIM
RWKV_HEADS = 4
RWKV_W = RWKV_HEADS * HEAD_DIM
POOL_GROUPS = 4
POOL_GROUP_DIM = 64
POOL_W = POOL_GROUPS * POOL_GROUP_DIM
POOL_WINDOWS = (2, 4, 8, 16)
POOL_HALO = 16
D_MIX = FOX_W + RWKV_W + POOL_W
RWKV_DECAY_RANK = 64
RWKV_ICL_RANK = 64
RWKV_SHIFT_W = 3 * RWKV_W + RWKV_DECAY_RANK + RWKV_ICL_RANK
NORM_EPS = 1e-6
RWKV_GN_EPS = 64e-5
DECAY_SCALE = float(math.exp(-0.5))
LOG2E = float(1.0 / math.log(2.0))

LANES = 128
F_PAD = LANES
CHUNK = 64
NEG_BIG = -1e30

VMEM_LIMIT = 56 * 1024 * 1024


def _cparams(sem):
    return pltpu.CompilerParams(dimension_semantics=sem, vmem_limit_bytes=VMEM_LIMIT)


def _dot(a, b):
    return jnp.dot(a, b, preferred_element_type=F32)


def _dot_nt(a, b):
    return lax.dot_general(a, b, (((1,), (1,)), ((), ())), preferred_element_type=F32)


def _dot_tn(a, b):
    return lax.dot_general(a, b, (((0,), (0,)), ((), ())), preferred_element_type=F32)


def _split2(x):
    hi = x.astype(BF16)
    lo = (x - hi.astype(F32)).astype(BF16)
    return hi, lo


def _split3(x):
    hi = x.astype(BF16)
    r = x - hi.astype(F32)
    mid = r.astype(BF16)
    lo = (r - mid.astype(F32)).astype(BF16)
    return hi, mid, lo


def _dot_exact_lhs(l_bf16, x):
    h, m, l = _split3(x)
    return _dot(l_bf16, h) + _dot(l_bf16, m) + _dot(l_bf16, l)


def _dot_wide_lhs(l_bf16, x):
    h, l = _split2(x)
    return _dot(l_bf16, h) + _dot(l_bf16, l)


def _dot_wide_rhs(x, r_bf16):
    h, l = _split2(x)
    return _dot(h, r_bf16) + _dot(l, r_bf16)


def _dot3(a, b, dot=_dot):
    ah, al = _split2(a)
    bh, bl = _split2(b)
    return dot(ah, bh) + dot(ah, bl) + dot(al, bh)


def _sigmoid(x):
    return 1.0 / (1.0 + jnp.exp(-x))


def _silu(x):
    return x * _sigmoid(x)


def _mod_kernel(c_ref, w_ref, b_ref, o_ref):
    sc = _silu(c_ref[...])
    o_ref[0] = jnp.dot(sc, w_ref[0], precision=lax.Precision.HIGHEST,
                       preferred_element_type=F32) + b_ref[0]


def _adaln_mod(c, ada_w, ada_b):
    L, D, D3 = ada_w.shape
    B = c.shape[0]
    nj = D3 // D
    return pl.pallas_call(
        _mod_kernel,
        grid=(L, nj),
        in_specs=[
            pl.BlockSpec((B, D), lambda l, j: (0, 0)),
            pl.BlockSpec((1, D, D), lambda l, j: (l, 0, j)),
            pl.BlockSpec((1, 1, D), lambda l, j: (l, 0, j)),
        ],
        out_specs=pl.BlockSpec((1, B, D), lambda l, j: (l, 0, j)),
        out_shape=jax.ShapeDtypeStruct((L, B, D3), F32),
        compiler_params=_cparams(("arbitrary", "arbitrary")),
        name="adaln_mod",
    )(c, ada_w, ada_b.reshape(L, 1, D3))


def _in_kernel(x_ref, mod_ref, npre_ref, wq_ref, wk_ref, wv_ref, wf_ref, wz_ref, wp_ref, wg_ref,
               fb_ref, qg_ref, kg_ref, ltri_ref,
               q_out, k_out, v_out, z_out, p_out, g_out, carry_ref):
    si = pl.program_id(1)

    @pl.when(si == 0)
    def _():
        carry_ref[...] = jnp.zeros_like(carry_ref)

    D = D_MODEL
    x = x_ref[0]
    ts = x.shape[0]
    ms = jnp.mean(x * x, axis=-1, keepdims=True)
    xn = x * lax.rsqrt(ms + NORM_EPS) * npre_ref[...]
    mod = mod_ref[0]
    shift = mod[:, 0:D]
    scale = mod[:, D:2 * D]
    h = (xn * (1.0 + scale) + shift).astype(BF16)

    z_out[0] = _dot(h, wz_ref[...])
    p_out[0] = _dot(h, wp_ref[...])
    g_out[0] = _dot(h, wg_ref[...])

    f = _dot(h, wf_ref[...]) + fb_ref[...]
    logf = (jnp.minimum(f, 0.0) - jnp.log1p(jnp.exp(-jnp.abs(f)))) * LOG2E
    cum = _dot_exact_lhs(ltri_ref[...], logf) + carry_ref[...]
    carry_ref[...] = cum[ts - 1:ts, :]
    c1 = cum.astype(BF16).astype(F32)
    r1 = cum - c1
    c2 = r1.astype(BF16).astype(F32)
    c3 = r1 - c2

    uq = _dot(h, wq_ref[...])
    uk = _dot(h, wk_ref[...])
    uv = _dot(h, wv_ref[...])

    lane = lax.broadcasted_iota(jnp.int32, (ts, LANES), 1)
    lo = lane < HEAD_DIM
    qscale = (HEAD_DIM ** -0.5) * LOG2E

    def head_norm(x2, gain2):
        sq = x2 * x2
        ss_lo = jnp.sum(jnp.where(lo, sq, 0.0), axis=-1, keepdims=True)
        ss_hi = jnp.sum(jnp.where(lo, 0.0, sq), axis=-1, keepdims=True)
        inv = jnp.where(lo, lax.rsqrt(ss_lo * (1.0 / HEAD_DIM) + NORM_EPS),
                        lax.rsqrt(ss_hi * (1.0 / HEAD_DIM) + NORM_EPS))
        return x2 * inv * gain2

    def extras(e0, hcol, for_q):
        a1 = c1[:, hcol:hcol + 1]
        a2 = c2[:, hcol:hcol + 1]
        a3 = c3[:, hcol:hcol + 1]
        one = jnp.ones((ts, 1), F32)
        if for_q:
            vals = (a1, a2, a3, one, one, one)
        else:
            vals = (one, one, one, -a1, -a2, -a3)
        e = jnp.zeros((ts, LANES), F32)
        for i, val in enumerate(vals):
            e = jnp.where(lane == e0 + i, val, e)
        return e

    for j in range(FOX_HEADS // 2):
        sl = slice(LANES * j, LANES * (j + 1))
        qn = head_norm(uq[:, sl], qg_ref[...]) * qscale
        kn = head_norm(uk[:, sl], kg_ref[...])
        v2 = uv[:, sl]
        he, ho = 2 * j, 2 * j + 1
        q_out[0, he] = jnp.where(lo, qn, extras(HEAD_DIM, he, True)).astype(BF16)
        k_out[0, he] = jnp.where(lo, kn, extras(HEAD_DIM, he, False)).astype(BF16)
        v_out[0, he] = jnp.where(lo, v2, jnp.where(lane == HEAD_DIM, 1.0, 0.0)).astype(BF16)
        q_out[0, ho] = jnp.where(lo, extras(0, ho, True), qn).astype(BF16)
        k_out[0, ho] = jnp.where(lo, extras(0, ho, False), kn).astype(BF16)
        v_out[0, ho] = jnp.where(lo, jnp.where(lane == 0, 1.0, 0.0), v2).astype(BF16)


def _in_proj(x, mod_l, npre, wq, wk, wv, wf, wz, wp, wg, fbias, qg2, kg2, ts):
    B, S, D = x.shape
    ns = S // ts
    ltri = jnp.tril(jnp.ones((ts, ts), F32)).astype(BF16)
    full = lambda a: pl.BlockSpec(a.shape, lambda b, s: (0,) * a.ndim)
    hs = jax.ShapeDtypeStruct((B, FOX_HEADS, S, LANES), BF16)
    head_spec = pl.BlockSpec((1, FOX_HEADS, ts, LANES), lambda b, s: (b, 0, s, 0))
    seq_spec = lambda w: pl.BlockSpec((1, ts, w), lambda b, s: (b, s, 0))
    return pl.pallas_call(
        _in_kernel,
        grid=(B, ns),
        in_specs=[
            seq_spec(D),
            pl.BlockSpec((1, 1, 3 * D), lambda b, s: (b, 0, 0)),
            full(npre), full(wq), full(wk), full(wv), full(wf), full(wz), full(wp), full(wg),
            full(fbias), full(qg2), full(kg2), full(ltri),
        ],
        out_specs=[head_spec, head_spec, head_spec,
                   seq_spec(RWKV_SHIFT_W), seq_spec(POOL_W), seq_spec(D_MIX)],
        out_shape=[hs, hs, hs,
                   jax.ShapeDtypeStruct((B, S, RWKV_SHIFT_W), F32),
                   jax.ShapeDtypeStruct((B, S, POOL_W), F32),
                   jax.ShapeDtypeStruct((B, S, D_MIX), F32)],
        scratch_shapes=[pltpu.VMEM((1, LANES), F32)],
        compiler_params=_cparams(("arbitrary", "arbitrary")),
        name="in_proj",
    )(x, mod_l, npre, wq, wk, wv, wf, wz, wp, wg, fbias, qg2, kg2, ltri)


def _attn_kernel(q_ref, k_ref, v_ref, o_ref, *, tk):
    qi = pl.program_id(2)
    tq = 2 * tk
    row = lax.broadcasted_iota(jnp.int32, (tq, tk), 0)
    col = lax.broadcasted_iota(jnp.int32, (tq, tk), 1)
    causal = col <= row
    lane = lax.broadcasted_iota(jnp.int32, (tq, LANES), 1)
    lo = lane < HEAD_DIM

    heads = range(2)

    def scores(hh, q, blk, w=tk):
        off = pl.multiple_of(blk * w, w)
        return _dot_nt(q, k_ref[0, hh, pl.ds(off, w), :])

    def update(hh, s, blk, m, acc, w=tk):
        off = pl.multiple_of(blk * w, w)
        m_new = jnp.maximum(m, jnp.max(s, axis=-1, keepdims=True))
        alpha = jnp.exp2(m - m_new)
        p = jnp.exp2(s - m_new)
        acc = alpha * acc + _dot(p.astype(BF16), v_ref[0, hh, pl.ds(off, w), :])
        return m_new, acc

    m0 = jnp.full((tq, 1), NEG_BIG, F32)
    a0 = jnp.zeros((tq, LANES), F32)

    def body(j, carry):
        ms, accs_ = carry
        ss = [scores(hh, q_ref[0, hh], j, tq) for hh in heads]
        upd = [update(hh, ss[hh], j, ms[hh], accs_[hh], tq) for hh in heads]
        return tuple(u[0] for u in upd), tuple(u[1] for u in upd)

    ms, accs = lax.fori_loop(0, qi, body, ((m0, m0), (a0, a0)))

    ss = [jnp.where(causal, scores(hh, q_ref[0, hh], 2 * qi), NEG_BIG) for hh in heads]
    upd = [update(hh, ss[hh], 2 * qi, ms[hh], accs[hh]) for hh in heads]
    ss = [jnp.where(causal[0:tk], scores(hh, q_ref[0, hh, tk:tq, :], 2 * qi + 1), NEG_BIG) for hh in heads]
    low = [update(hh, ss[hh], 2 * qi + 1, upd[hh][0][tk:tq], upd[hh][1][tk:tq])[1] for hh in heads]
    accs = [jnp.concatenate([upd[hh][1][0:tk], low[hh]], axis=0) for hh in heads]

    l_e = accs[0][:, HEAD_DIM:HEAD_DIM + 1]
    l_o = accs[1][:, 0:1]
    o_ref[0] = jnp.where(lo, accs[0] / l_e, accs[1] / l_o)


def _fox_attention(q, k, v, tk):
    B, H, S, _ = q.shape
    tq = 2 * tk
    nq = S // tq
    return pl.pallas_call(
        functools.partial(_attn_kernel, tk=tk),
        grid=(B, H // 2, nq),
        in_specs=[
            pl.BlockSpec((1, 2, tq, LANES), lambda b, p, i: (b, p, i, 0)),
            pl.BlockSpec((1, 2, S, LANES), lambda b, p, i: (b, p, 0, 0)),
            pl.BlockSpec((1, 2, S, LANES), lambda b, p, i: (b, p, 0, 0)),
        ],
        out_specs=pl.BlockSpec((1, tq, LANES), lambda b, p, i: (b, i, p)),
        out_shape=jax.ShapeDtypeStruct((B, S, FOX_W), F32),
        compiler_params=_cparams(("arbitrary", "arbitrary", "arbitrary")),
        name="fox_attention",
    )(q, k, v)


def _mm(a, b):
    return _dot(a.astype(BF16), b.astype(BF16))


def _mm_nt(a, b):
    return _dot_nt(a.astype(BF16), b.astype(BF16))


def _mm_tn(a, b):
    return _dot_tn(a.astype(BF16), b.astype(BF16))


def _unit_lower_inverses(ms, row, col):
    eye = (row == col).astype(F32)
    same = lambda sh: (row >> sh) == (col >> sh)
    ts = [eye + jnp.where(same(1), m, 0.0) for m in ms]
    for sh in range(1, 6):
        band = jnp.logical_and(same(sh + 1), jnp.logical_not(same(sh)))
        ets = [_mm(jnp.where(band, m, 0.0), t) for m, t in zip(ms, ts)]
        ts = [t + _mm(t, et) for t, et in zip(ts, ets)]
    return ts


def _rwkv_kernel(z_ref, mu_ref, w0_ref, a0_ref, kk_ref, ka_ref, rk_ref, lng_ref, lnb_ref,
                 w2a2_ref, seg_ref, ltri_ref, y_ref, state_ref, prev_ref):
    si = pl.program_id(1)

    @pl.when(si == 0)
    def _():
        state_ref[...] = jnp.zeros_like(state_ref)
        prev_ref[...] = jnp.zeros_like(prev_ref)

    W = RWKV_W
    N = HEAD_DIM
    z = z_ref[0]
    cb = z.shape[0]
    row = lax.broadcasted_iota(jnp.int32, (cb, 1), 0)
    zprev = jnp.where(row == 0, prev_ref[...], pltpu.roll(z, 1, 0))
    prev_ref[...] = z[cb - 1:cb, :]
    zf = z + (zprev - z) * mu_ref[...]
    r = zf[:, 0:W]
    k = zf[:, W:2 * W]
    v = zf[:, 2 * W:3 * W]
    t6 = zf[:, 3 * W:3 * W + 2 * N]
    lane = lax.broadcasted_iota(jnp.int32, (cb, 2 * N), 1)
    tz = jnp.where(lane < N, jnp.tanh(t6), t6)
    dwa = _dot3(tz, w2a2_ref[...])
    logw = -DECAY_SCALE * _sigmoid(w0_ref[...] + dwa[:, 0:W])
    a = _sigmoid(a0_ref[...] + dwa[:, W:2 * W])

    seg = seg_ref[...]
    kkr = k * kk_ref[...]
    kk = kkr / jnp.maximum(jnp.sqrt(_dot_wide_rhs(kkr * kkr, seg)), 1e-12)
    k2 = k * (1.0 + (a - 1.0) * ka_ref[...])
    bonus = _dot_wide_rhs(r * k2 * rk_ref[...], seg) * v

    logg = _dot_exact_lhs(ltri_ref[...], logw)
    g = jnp.exp(logg)
    gex = jnp.exp(logg - logw)
    gi = jnp.exp(-logg)
    at = -kk * gex
    rt = r * g
    bt = kk * a * gi
    kt = k2 * gi

    rw = lax.broadcasted_iota(jnp.int32, (W, W), 0)
    cw = lax.broadcasted_iota(jnp.int32, (W, W), 1)
    head_blk = (rw >> 6) == (cw >> 6)
    tt = rw & (CHUNK - 1)
    ss = cw & (CHUNK - 1)
    strict = jnp.logical_and(head_blk, ss < tt)
    incl = jnp.logical_and(head_blk, ss <= tt)

    def tall(zc):
        return jnp.where(head_blk, jnp.concatenate([zc] * RWKV_HEADS, axis=0), 0.0).astype(BF16)

    def fold(y):
        return y[0:CHUNK] + y[CHUNK:2 * CHUNK] + y[2 * CHUNK:3 * CHUNK] + y[3 * CHUNK:4 * CHUNK]

    nchunk = cb // CHUNK
    chunks = range(nchunk)
    rows = [slice(CHUNK * c, CHUNK * (c + 1)) for c in chunks]
    glast = [g[CHUNK * (c + 1) - 1:CHUNK * (c + 1), :] for c in chunks]
    t_a = [tall(at[rs]) for rs in rows]
    t_r = [tall(rt[rs]) for rs in rows]
    t_b = [tall(bt[rs]) for rs in rows]
    t_k = [tall(kt[rs]) for rs in rows]
    t_v = [tall(v[rs]) for rs in rows]
    pp = [_dot_nt(jnp.concatenate([t_a[c], t_r[c]], axis=0), jnp.concatenate([t_b[c], t_k[c]], axis=0))
          for c in chunks]
    m_ab = [jnp.where(strict, p[0:W, 0:W], 0.0) for p in pp]
    m_ak = [jnp.where(strict, p[0:W, W:2 * W], 0.0) for p in pp]
    m_rb = [jnp.where(incl, p[W:2 * W, 0:W], 0.0).astype(BF16) for p in pp]
    m_rk = [jnp.where(incl, p[W:2 * W, W:2 * W], 0.0) for p in pp]
    xo = [_mm(jnp.concatenate([m_ak[c], m_rk[c]], axis=0), t_v[c]) for c in chunks]
    d0 = [jnp.where(head_blk, _mm_tn(v[rows[c]], kt[rows[c]] * glast[c]), 0.0) for c in chunks]
    bh = [(bt[rows[c]] * glast[c]).astype(BF16) for c in chunks]
    tinv = _unit_lower_inverses(m_ab, rw, cw)
    wu = [_mm(tinv[c], jnp.concatenate([t_a[c], xo[c][0:W].astype(BF16)], axis=1)) for c in chunks]
    w_all = [fold(x[:, 0:W]) for x in wu]
    u0_all = [fold(x[:, W:2 * W]) for x in wu]
    o0_all = [fold(x[W:2 * W]) for x in xo]
    wu_bf = [jnp.concatenate([w_all[c], u0_all[c]], axis=1).astype(BF16) for c in chunks]
    pq = [_dot_tn(wu_bf[c], bh[c]) for c in chunks]
    phi = [jnp.where(head_blk, x[0:W], 0.0).astype(BF16) for x in pq]
    psi = [jnp.where(head_blk, pq[c][W:2 * W], 0.0) + d0[c] for c in chunks]
    mw = [_dot(m_rb[c], jnp.concatenate([tall(w_all[c]), tall(u0_all[c])], axis=1)) for c in chunks]
    omega = [(rt[rows[c]] + fold(mw[c][:, 0:W])).astype(BF16) for c in chunks]
    xi = [fold(mw[c][:, W:2 * W]) + o0_all[c] for c in chunks]

    outs = []
    s_bd = state_ref[...]
    for c in chunks:
        s_bf = s_bd.astype(BF16)
        outs.append(_dot_nt(omega[c], s_bf) + xi[c])
        s_bd = s_bd * glast[c] + _dot(s_bf, phi[c]) + psi[c]
    state_ref[...] = s_bd

    o = jnp.concatenate(outs, axis=0)
    mean = _dot_wide_rhs(o, seg) * (1.0 / N)
    d = o - mean
    var = _dot_wide_rhs(d * d, seg) * (1.0 / N)
    y_ref[0] = d * lax.rsqrt(var + RWKV_GN_EPS) * lng_ref[...] + lnb_ref[...] + bonus


def _rwkv(z, mu, w0, a0, k_k, k_a, r_k, ln_g, ln_b, w2a2, cb):
    B, S, _ = z.shape
    ns = S // cb
    seg = jnp.kron(jnp.eye(RWKV_HEADS, dtype=F32), jnp.ones((HEAD_DIM, HEAD_DIM), F32)).astype(BF16)
    ltri = jnp.kron(jnp.eye(cb // CHUNK, dtype=F32), jnp.tril(jnp.ones((CHUNK, CHUNK), F32))).astype(BF16)
    full = lambda a: pl.BlockSpec(a.shape, lambda b, s: (0,) * a.ndim)
    args = (mu, w0, a0, k_k, k_a, r_k, ln_g, ln_b, w2a2, seg, ltri)
    return pl.pallas_call(
        _rwkv_kernel,
        grid=(B, ns),
        in_specs=[pl.BlockSpec((1, cb, RWKV_SHIFT_W), lambda b, s: (b, s, 0))] + [full(a) for a in args],
        out_specs=pl.BlockSpec((1, cb, RWKV_W), lambda b, s: (b, s, 0)),
        out_shape=jax.ShapeDtypeStruct((B, S, RWKV_W), F32),
        scratch_shapes=[pltpu.VMEM((RWKV_W, RWKV_W), F32),
                        pltpu.VMEM((1, RWKV_SHIFT_W), F32)],
        compiler_params=_cparams(("arbitrary", "arbitrary")),
        name="rwkv7",
    )(z, *args)


def _out_kernel(x_ref, mod_ref, yf_ref, yr_ref, p_ref, g_ref, pw_ref, ps_ref, wout_ref, npost_ref,
                o_ref, halo_ref):
    si = pl.program_id(1)

    @pl.when(si == 0)
    def _():
        halo_ref[...] = jnp.zeros_like(halo_ref)

    D = D_MODEL
    p = p_ref[0]
    ts = p.shape[0]
    e = jnp.concatenate([halo_ref[...], p], axis=0)
    halo_ref[...] = p[ts - POOL_HALO:ts, :]
    sums = []
    s_run = e
    shift = 1
    for _w in POOL_WINDOWS:
        s_run = s_run + pltpu.roll(s_run, shift, 0)
        sums.append(s_run[POOL_HALO:, :])
        shift *= 2
    pos = si * ts + lax.broadcasted_iota(jnp.int32, (ts, 1), 0) + 1
    lane = lax.broadcasted_iota(jnp.int32, (ts, POOL_W), 1)
    pooled = jnp.zeros((ts, POOL_W), F32)
    for gi, w in enumerate(POOL_WINDOWS):
        cnt = jnp.minimum(pos, w).astype(F32)
        grp = (lane >= POOL_GROUP_DIM * gi) & (lane < POOL_GROUP_DIM * (gi + 1))
        pooled = jnp.where(grp, sums[gi] / cnt, pooled)
    pooled = pooled - p
    y_pool = _dot(pooled.astype(BF16), pw_ref[...]) * ps_ref[...]

    y = jnp.concatenate([yf_ref[0], yr_ref[0], y_pool], axis=-1) * _silu(g_ref[0])
    t = _dot(y.astype(BF16), wout_ref[...])
    ms = jnp.mean(t * t, axis=-1, keepdims=True)
    tn = t * lax.rsqrt(ms + NORM_EPS) * npost_ref[...]
    gate = mod_ref[0][:, 2 * D:3 * D]
    o_ref[0] = x_ref[0] + gate * tn


def _out_proj(x, mod_l, y_fox, y_rwkv, p, g, pw_bd, pool_scale, w_out, npost, ts):
    B, S, D = x.shape
    ns = S // ts
    full = lambda a: pl.BlockSpec(a.shape, lambda b, s: (0,) * a.ndim)
    seq_spec = lambda w: pl.BlockSpec((1, ts, w), lambda b, s: (b, s, 0))
    return pl.pallas_call(
        _out_kernel,
        grid=(B, ns),
        in_specs=[seq_spec(D), pl.BlockSpec((1, 1, 3 * D), lambda b, s: (b, 0, 0)),
                  seq_spec(FOX_W), seq_spec(RWKV_W), seq_spec(POOL_W), seq_spec(D_MIX),
                  full(pw_bd), full(pool_scale), full(w_out), full(npost)],
        out_specs=seq_spec(D),
        out_shape=jax.ShapeDtypeStruct((B, S, D), F32),
        scratch_shapes=[pltpu.VMEM((POOL_HALO, POOL_W), F32)],
        compiler_params=_cparams(("arbitrary", "arbitrary")),
        name="out_proj",
    )(x, mod_l, y_fox, y_rwkv, p, g, pw_bd, pool_scale, w_out, npost)


def _block_diag(blocks):
    n = blocks.shape[0]
    d = blocks.shape[1]
    out = jnp.zeros((n * d, n * d), blocks.dtype)
    for i in range(n):
        out = out.at[i * d:(i + 1) * d, i * d:(i + 1) * d].set(blocks[i])
    return out


def kernel(x, c, ada_w, ada_b, norm_pre, norm_post, w_in, fox_q_gain, fox_k_gain, fox_f_bias, rwkv_mu, rwkv_w0, rwkv_w2, rwkv_a0, rwkv_a2, rwkv_k_k, rwkv_k_a, rwkv_r_k, rwkv_ln_g, rwkv_ln_b, pool_w, pool_scale, w_out):
    B, S, D = x.shape
    L = ada_w.shape[0]
    ts = min(512, S)
    tk = min(512, S // 2)
    cb = min(256, S)

    mod = _adaln_mod(c, ada_w, ada_b)
    o_q, o_k, o_v, o_f, o_z, o_p = (0, FOX_W, 2 * FOX_W, 3 * FOX_W, 3 * FOX_W + FOX_HEADS,
                                    3 * FOX_W + FOX_HEADS + RWKV_SHIFT_W)
    o_g = o_p + POOL_W
    row = lambda a: a.reshape(1, -1)

    for l in range(L):
        w = w_in[l]
        wq = w[:, o_q:o_k].astype(BF16)
        wk = w[:, o_k:o_v].astype(BF16)
        wv = w[:, o_v:o_f].astype(BF16)
        wf = jnp.pad(w[:, o_f:o_z], ((0, 0), (0, F_PAD - FOX_HEADS))).astype(BF16)
        wz = w[:, o_z:o_p].astype(BF16)
        wp = w[:, o_p:o_g].astype(BF16)
        wg = w[:, o_g:].astype(BF16)
        fbias = jnp.pad(fox_f_bias[l], (0, F_PAD - FOX_HEADS)).reshape(1, F_PAD)
        qg2 = jnp.tile(fox_q_gain[l], 2).reshape(1, LANES)
        kg2 = jnp.tile(fox_k_gain[l], 2).reshape(1, LANES)
        mod_l = mod[l].reshape(B, 1, 3 * D)

        q_aug, k_aug, v_aug, z, p, g = _in_proj(x, mod_l, row(norm_pre[l]), wq, wk, wv, wf, wz, wp, wg,
                                                fbias, qg2, kg2, ts)
        y_fox = _fox_attention(q_aug, k_aug, v_aug, tk)

        w2a2 = jnp.zeros((2 * HEAD_DIM, 2 * RWKV_W), F32)
        w2a2 = w2a2.at[:RWKV_DECAY_RANK, :RWKV_W].set(rwkv_w2[l])
        w2a2 = w2a2.at[RWKV_DECAY_RANK:, RWKV_W:].set(rwkv_a2[l])
        y_rwkv = _rwkv(z, row(rwkv_mu[l]), row(rwkv_w0[l]), row(rwkv_a0[l]), row(rwkv_k_k[l]),
                       row(rwkv_k_a[l]), row(rwkv_r_k[l]), row(rwkv_ln_g[l]), row(rwkv_ln_b[l]),
                       w2a2, cb)

        pw_bd = _block_diag(pool_w[l]).astype(BF16)
        x = _out_proj(x, mod_l, y_fox, y_rwkv, p, g, pw_bd, row(pool_scale[l]),
                      w_out[l].astype(BF16), row(norm_post[l]), ts)
    return x
```

```python
import functools
import math

import jax
import jax.numpy as jnp
from jax import lax
from jax.experimental import pallas as pl
from jax.experimental.pallas import tpu as pltpu

F32 = jnp.float32
BF16 = jnp.bfloat16

D_MODEL = 1024
HEAD_DIM = 64
FOX_HEADS = 8
FOX_W = FOX_HEADS * HEAD_DIM
RWKV_HEADS = 4
RWKV_W = RWKV_HEADS * HEAD_DIM
POOL_GROUPS = 4
POOL_GROUP_DIM = 64
POOL_W = POOL_GROUPS * POOL_GROUP_DIM
POOL_WINDOWS = (2, 4, 8, 16)
POOL_HALO = 16
D_MIX = FOX_W + RWKV_W + POOL_W
RWKV_DECAY_RANK = 64
RWKV_ICL_RANK = 64
RWKV_SHIFT_W = 3 * RWKV_W + RWKV_DECAY_RANK + RWKV_ICL_RANK
NORM_EPS = 1e-6
RWKV_GN_EPS = 64e-5
DECAY_SCALE = float(math.exp(-0.5))
LOG2E = float(1.0 / math.log(2.0))

LANES = 128
F_PAD = LANES
CHUNK = 64
NEG_BIG = -1e30

VMEM_LIMIT = 56 * 1024 * 1024


def _cparams(sem):
    return pltpu.CompilerParams(dimension_semantics=sem, vmem_limit_bytes=VMEM_LIMIT)


def _dot(a, b):
    return jnp.dot(a, b, preferred_element_type=F32)


def _dot_nt(a, b):
    return lax.dot_general(a, b, (((1,), (1,)), ((), ())), preferred_element_type=F32)


def _dot_tn(a, b):
    return lax.dot_general(a, b, (((0,), (0,)), ((), ())), preferred_element_type=F32)


def _split2(x):
    hi = x.astype(BF16)
    lo = (x - hi.astype(F32)).astype(BF16)
    return hi, lo


def _split3(x):
    hi = x.astype(BF16)
    r = x - hi.astype(F32)
    mid = r.astype(BF16)
    lo = (r - mid.astype(F32)).astype(BF16)
    return hi, mid, lo


def _dot_exact_lhs(l_bf16, x):
    h, m, l = _split3(x)
    return _dot(l_bf16, h) + _dot(l_bf16, m) + _dot(l_bf16, l)


def _dot_wide_lhs(l_bf16, x):
    h, l = _split2(x)
    return _dot(l_bf16, h) + _dot(l_bf16, l)


def _dot_wide_rhs(x, r_bf16):
    h, l = _split2(x)
    return _dot(h, r_bf16) + _dot(l, r_bf16)


def _dot3(a, b, dot=_dot):
    ah, al = _split2(a)
    bh, bl = _split2(b)
    return dot(ah, bh) + dot(ah, bl) + dot(al, bh)


def _sigmoid(x):
    return 1.0 / (1.0 + jnp.exp(-x))


def _silu(x):
    return x * _sigmoid(x)


def _mod_kernel(c_ref, w_ref, b_ref, o_ref):
    sc = _silu(c_ref[...])
    o_ref[0] = jnp.dot(sc, w_ref[0], precision=lax.Precision.HIGHEST,
                       preferred_element_type=F32) + b_ref[0]


def _adaln_mod(c, ada_w, ada_b):
    L, D, D3 = ada_w.shape
    B = c.shape[0]
    nj = D3 // D
    return pl.pallas_call(
        _mod_kernel,
        grid=(L, nj),
        in_specs=[
            pl.BlockSpec((B, D), lambda l, j: (0, 0)),
            pl.BlockSpec((1, D, D), lambda l, j: (l, 0, j)),
            pl.BlockSpec((1, 1, D), lambda l, j: (l, 0, j)),
        ],
        out_specs=pl.BlockSpec((1, B, D), lambda l, j: (l, 0, j)),
        out_shape=jax.ShapeDtypeStruct((L, B, D3), F32),
        compiler_params=_cparams(("arbitrary", "arbitrary")),
        name="adaln_mod",
    )(c, ada_w, ada_b.reshape(L, 1, D3))


def _in_kernel(x_ref, mod_ref, npre_ref, wq_ref, wk_ref, wv_ref, wf_ref, wz_ref, wp_ref, wg_ref,
               fb_ref, qg_ref, kg_ref, ltri_ref,
               q_out, k_out, v_out, z_out, p_out, g_out, carry_ref):
    si = pl.program_id(1)

    @pl.when(si == 0)
    def _():
        carry_ref[...] = jnp.zeros_like(carry_ref)

    D = D_MODEL
    x = x_ref[0]
    ts = x.shape[0]
    ms = jnp.mean(x * x, axis=-1, keepdims=True)
    xn = x * lax.rsqrt(ms + NORM_EPS) * npre_ref[...]
    mod = mod_ref[0]
    shift = mod[:, 0:D]
    scale = mod[:, D:2 * D]
    h = (xn * (1.0 + scale) + shift).astype(BF16)

    f = _dot(h, wf_ref[...]) + fb_ref[...]
    logf = (jnp.minimum(f, 0.0) - jnp.log1p(jnp.exp(-jnp.abs(f)))) * LOG2E
    cum = _dot_exact_lhs(ltri_ref[...], logf) + carry_ref[...]
    carry_ref[...] = cum[ts - 1:ts, :]
    c1 = cum.astype(BF16).astype(F32)
    r1 = cum - c1
    c2 = r1.astype(BF16).astype(F32)
    c3 = r1 - c2

    uq = _dot(h, wq_ref[...])
    uk = _dot(h, wk_ref[...])
    uv = _dot(h, wv_ref[...])

    lane = lax.broadcasted_iota(jnp.int32, (ts, LANES), 1)
    lo = lane < HEAD_DIM
    qscale = (HEAD_DIM ** -0.5) * LOG2E

    def head_norm(x2, gain2):
        sq = x2 * x2
        ss_lo = jnp.sum(jnp.where(lo, sq, 0.0), axis=-1, keepdims=True)
        ss_hi = jnp.sum(jnp.where(lo, 0.0, sq), axis=-1, keepdims=True)
        inv = jnp.where(lo, lax.rsqrt(ss_lo * (1.0 / HEAD_DIM) + NORM_EPS),
                        lax.rsqrt(ss_hi * (1.0 / HEAD_DIM) + NORM_EPS))
        return x2 * inv * gain2

    def extras(e0, hcol, for_q):
        a1 = c1[:, hcol:hcol + 1]
        a2 = c2[:, hcol:hcol + 1]
        a3 = c3[:, hcol:hcol + 1]
        one = jnp.ones((ts, 1), F32)
        if for_q:
            vals = (a1, a2, a3, one, one, one)
        else:
            vals = (one, one, one, -a1, -a2, -a3)
        e = jnp.zeros((ts, LANES), F32)
        for i, val in enumerate(vals):
            e = jnp.where(lane == e0 + i, val, e)
        return e

    for j in range(FOX_HEADS // 2):
        sl = slice(LANES * j, LANES * (j + 1))
        qn = head_norm(uq[:, sl], qg_ref[...]) * qscale
        kn = head_norm(uk[:, sl], kg_ref[...])
        v2 = uv[:, sl]
        he, ho = 2 * j, 2 * j + 1
        q_out[0, he] = jnp.where(lo, qn, extras(HEAD_DIM, he, True)).astype(BF16)
        k_out[0, he] = jnp.where(lo, kn, extras(HEAD_DIM, he, False)).astype(BF16)
        v_out[0, he] = jnp.where(lo, v2, jnp.where(lane == HEAD_DIM, 1.0, 0.0)).astype(BF16)
        q_out[0, ho] = jnp.where(lo, extras(0, ho, True), qn).astype(BF16)
        k_out[0, ho] = jnp.where(lo, extras(0, ho, False), kn).astype(BF16)
        v_out[0, ho] = jnp.where(lo, jnp.where(lane == 0, 1.0, 0.0), v2).astype(BF16)

    z_out[0] = _dot(h, wz_ref[...])
    p_out[0] = _dot(h, wp_ref[...])
    g_out[0] = _dot(h, wg_ref[...]).astype(BF16)


def _in_proj(x, mod_l, npre, wq, wk, wv, wf, wz, wp, wg, fbias, qg2, kg2, ts):
    B, S, D = x.shape
    ns = S // ts
    ltri = jnp.tril(jnp.ones((ts, ts), F32)).astype(BF16)
    full = lambda a: pl.BlockSpec(a.shape, lambda b, s: (0,) * a.ndim)
    hs = jax.ShapeDtypeStruct((B, FOX_HEADS, S, LANES), BF16)
    head_spec = pl.BlockSpec((1, FOX_HEADS, ts, LANES), lambda b, s: (b, 0, s, 0))
    seq_spec = lambda w: pl.BlockSpec((1, ts, w), lambda b, s: (b, s, 0))
    return pl.pallas_call(
        _in_kernel,
        grid=(B, ns),
        in_specs=[
            seq_spec(D),
            pl.BlockSpec((1, 1, 3 * D), lambda b, s: (b, 0, 0)),
            full(npre), full(wq), full(wk), full(wv), full(wf), full(wz), full(wp), full(wg),
            full(fbias), full(qg2), full(kg2), full(ltri),
        ],
        out_specs=[head_spec, head_spec, head_spec,
                   seq_spec(RWKV_SHIFT_W), seq_spec(POOL_W), seq_spec(D_MIX)],
        out_shape=[hs, hs, hs,
                   jax.ShapeDtypeStruct((B, S, RWKV_SHIFT_W), F32),
                   jax.ShapeDtypeStruct((B, S, POOL_W), F32),
                   jax.ShapeDtypeStruct((B, S, D_MIX), BF16)],
        scratch_shapes=[pltpu.VMEM((1, LANES), F32)],
        compiler_params=_cparams(("arbitrary", "arbitrary")),
        name="in_proj",
    )(x, mod_l, npre, wq, wk, wv, wf, wz, wp, wg, fbias, qg2, kg2, ltri)


def _attn_kernel(q_ref, k_ref, v_ref, o_ref, *, tk):
    qi = pl.program_id(2)
    tq = 2 * tk
    row = lax.broadcasted_iota(jnp.int32, (tq, tk), 0)
    col = lax.broadcasted_iota(jnp.int32, (tq, tk), 1)
    causal = col <= row
    lane = lax.broadcasted_iota(jnp.int32, (tq, LANES), 1)
    lo = lane < HEAD_DIM

    heads = range(2)

    def scores(hh, q, blk, w=tk):
        off = pl.multiple_of(blk * w, w)
        return _dot_nt(q, k_ref[0, hh, pl.ds(off, w), :])

    def update(hh, s, blk, m, acc, w=tk):
        off = pl.multiple_of(blk * w, w)
        m_new = jnp.maximum(m, jnp.max(s, axis=-1, keepdims=True))
        alpha = jnp.exp2(m - m_new)
        p = jnp.exp2(s - m_new)
        acc = alpha * acc + _dot(p.astype(BF16), v_ref[0, hh, pl.ds(off, w), :])
        return m_new, acc

    m0 = jnp.full((tq, 1), NEG_BIG, F32)
    a0 = jnp.zeros((tq, LANES), F32)

    def body(j, carry):
        ms, accs_ = carry
        ss = [scores(hh, q_ref[0, hh], j, tq) for hh in heads]
        upd = [update(hh, ss[hh], j, ms[hh], accs_[hh], tq) for hh in heads]
        return tuple(u[0] for u in upd), tuple(u[1] for u in upd)

    ms, accs = lax.fori_loop(0, qi, body, ((m0, m0), (a0, a0)))

    row_b = lax.broadcasted_iota(jnp.int32, (tk, tq), 0)
    col_b = lax.broadcasted_iota(jnp.int32, (tk, tq), 1)
    causal_b = col_b <= row_b + tk
    s_up = [jnp.where(causal[0:tk], scores(hh, q_ref[0, hh, 0:tk, :], 2 * qi), NEG_BIG) for hh in heads]
    s_lo = [jnp.where(causal_b, scores(hh, q_ref[0, hh, tk:tq, :], qi, tq), NEG_BIG) for hh in heads]
    up = [update(hh, s_up[hh], 2 * qi, ms[hh][0:tk], accs[hh][0:tk])[1] for hh in heads]
    low = [update(hh, s_lo[hh], qi, ms[hh][tk:tq], accs[hh][tk:tq], tq)[1] for hh in heads]
    accs = [jnp.concatenate([up[hh], low[hh]], axis=0) for hh in heads]

    l_e = accs[0][:, HEAD_DIM:HEAD_DIM + 1]
    l_o = accs[1][:, 0:1]
    o_ref[0] = jnp.where(lo, accs[0] / l_e, accs[1] / l_o).astype(o_ref.dtype)


def _fox_attention(q, k, v, tk):
    B, H, S, _ = q.shape
    tq = 2 * tk
    nq = S // tq
    return pl.pallas_call(
        functools.partial(_attn_kernel, tk=tk),
        grid=(B, H // 2, nq),
        in_specs=[
            pl.BlockSpec((1, 2, tq, LANES), lambda b, p, i: (b, p, i, 0)),
            pl.BlockSpec((1, 2, S, LANES), lambda b, p, i: (b, p, 0, 0)),
            pl.BlockSpec((1, 2, S, LANES), lambda b, p, i: (b, p, 0, 0)),
        ],
        out_specs=pl.BlockSpec((1, tq, LANES), lambda b, p, i: (b, i, p)),
        out_shape=jax.ShapeDtypeStruct((B, S, FOX_W), BF16),
        compiler_params=_cparams(("arbitrary", "arbitrary", "arbitrary")),
        name="fox_attention",
    )(q, k, v)


def _mm(a, b):
    return _dot(a.astype(BF16), b.astype(BF16))


def _mm_nt(a, b):
    return _dot_nt(a.astype(BF16), b.astype(BF16))


def _mm_tn(a, b):
    return _dot_tn(a.astype(BF16), b.astype(BF16))


def _unit_lower_inverses(ms, row, col):
    eye = (row == col).astype(F32)
    same = lambda sh: (row >> sh) == (col >> sh)
    ts = [eye + jnp.where(same(1), m, 0.0) for m in ms]
    for sh in range(1, 6):
        band = jnp.logical_and(same(sh + 1), jnp.logical_not(same(sh)))
        ets = [_mm(jnp.where(band, m, 0.0), t) for m, t in zip(ms, ts)]
        ts = [t + _mm(t, et) for t, et in zip(ts, ets)]
    return ts


def _rwkv_kernel(z_ref, mu_ref, w0_ref, a0_ref, kk_ref, ka_ref, rk_ref, lng_ref, lnb_ref,
                 w2a2_ref, seg_ref, ltri_ref, y_ref, state_ref, prev_ref):
    si = pl.program_id(1)

    @pl.when(si == 0)
    def _():
        state_ref[...] = jnp.zeros_like(state_ref)
        prev_ref[...] = jnp.zeros_like(prev_ref)

    W = RWKV_W
    N = HEAD_DIM
    z = z_ref[0]
    cb = z.shape[0]
    row = lax.broadcasted_iota(jnp.int32, (cb, 1), 0)
    zprev = jnp.where(row == 0, prev_ref[...], pltpu.roll(z, 1, 0))
    prev_ref[...] = z[cb - 1:cb, :]
    zf = z + (zprev - z) * mu_ref[...]
    r = zf[:, 0:W]
    k = zf[:, W:2 * W]
    v = zf[:, 2 * W:3 * W]
    t6 = zf[:, 3 * W:3 * W + 2 * N]
    lane = lax.broadcasted_iota(jnp.int32, (cb, 2 * N), 1)
    tz = jnp.where(lane < N, jnp.tanh(t6), t6)
    dwa = _dot3(tz, w2a2_ref[...])
    logw = -DECAY_SCALE * _sigmoid(w0_ref[...] + dwa[:, 0:W])
    a = _sigmoid(a0_ref[...] + dwa[:, W:2 * W])

    seg = seg_ref[...]
    kkr = k * kk_ref[...]
    kk = kkr / jnp.maximum(jnp.sqrt(_dot_wide_rhs(kkr * kkr, seg)), 1e-12)
    k2 = k * (1.0 + (a - 1.0) * ka_ref[...])
    bonus = _dot_wide_rhs(r * k2 * rk_ref[...], seg) * v

    logg = _dot_exact_lhs(ltri_ref[...], logw)
    g = jnp.exp(logg)
    gex = jnp.exp(logg - logw)
    gi = jnp.exp(-logg)
    at = -kk * gex
    rt = r * g
    bt = kk * a * gi
    kt = k2 * gi

    rw = lax.broadcasted_iota(jnp.int32, (W, W), 0)
    cw = lax.broadcasted_iota(jnp.int32, (W, W), 1)
    head_blk = (rw >> 6) == (cw >> 6)
    tt = rw & (CHUNK - 1)
    ss = cw & (CHUNK - 1)
    strict = jnp.logical_and(head_blk, ss < tt)
    incl = jnp.logical_and(head_blk, ss <= tt)

    def tall(zc):
        return jnp.where(head_blk, jnp.concatenate([zc] * RWKV_HEADS, axis=0), 0.0).astype(BF16)

    def fold(y):
        return y[0:CHUNK] + y[CHUNK:2 * CHUNK] + y[2 * CHUNK:3 * CHUNK] + y[3 * CHUNK:4 * CHUNK]

    nchunk = cb // CHUNK
    chunks = range(nchunk)
    rows = [slice(CHUNK * c, CHUNK * (c + 1)) for c in chunks]
    glast = [g[CHUNK * (c + 1) - 1:CHUNK * (c + 1), :] for c in chunks]
    t_a = [tall(at[rs]) for rs in rows]
    t_r = [tall(rt[rs]) for rs in rows]
    t_b = [tall(bt[rs]) for rs in rows]
    t_k = [tall(kt[rs]) for rs in rows]
    t_v = [tall(v[rs]) for rs in rows]
    pp = [_dot_nt(jnp.concatenate([t_a[c], t_r[c]], axis=0), jnp.concatenate([t_b[c], t_k[c]], axis=0))
          for c in chunks]
    m_ab = [jnp.where(strict, p[0:W, 0:W], 0.0) for p in pp]
    m_ak = [jnp.where(strict, p[0:W, W:2 * W], 0.0) for p in pp]
    m_rb = [jnp.where(incl, p[W:2 * W, 0:W], 0.0).astype(BF16) for p in pp]
    m_rk = [jnp.where(incl, p[W:2 * W, W:2 * W], 0.0) for p in pp]
    xo = [_mm(jnp.concatenate([m_ak[c], m_rk[c]], axis=0), t_v[c]) for c in chunks]
    d0 = [jnp.where(head_blk, _mm_tn(v[rows[c]], kt[rows[c]] * glast[c]), 0.0) for c in chunks]
    bh = [(bt[rows[c]] * glast[c]).astype(BF16) for c in chunks]
    tinv = _unit_lower_inverses(m_ab, rw, cw)
    wu = [_mm(tinv[c], jnp.concatenate([t_a[c], xo[c][0:W].astype(BF16)], axis=1)) for c in chunks]
    w_all = [fold(x[:, 0:W]) for x in wu]
    u0_all = [fold(x[:, W:2 * W]) for x in wu]
    o0_all = [fold(x[W:2 * W]) for x in xo]
    wu_bf = [jnp.concatenate([w_all[c], u0_all[c]], axis=1).astype(BF16) for c in chunks]
    pq = [_dot_tn(wu_bf[c], bh[c]) for c in chunks]
    phi = [jnp.where(head_blk, x[0:W], 0.0).astype(BF16) for x in pq]
    psi = [jnp.where(head_blk, pq[c][W:2 * W], 0.0) + d0[c] for c in chunks]
    mw = [_dot(m_rb[c], jnp.concatenate([tall(w_all[c]), tall(u0_all[c])], axis=1)) for c in chunks]
    omega = [(rt[rows[c]] + fold(mw[c][:, 0:W])).astype(BF16) for c in chunks]
    xi = [fold(mw[c][:, W:2 * W]) + o0_all[c] for c in chunks]

    outs = []
    s_bd = state_ref[...]
    for c in chunks:
        s_bf = s_bd.astype(BF16)
        outs.append(_dot_nt(omega[c], s_bf) + xi[c])
        s_bd = s_bd * glast[c] + _dot(s_bf, phi[c]) + psi[c]
    state_ref[...] = s_bd

    o = jnp.concatenate(outs, axis=0)
    mean = _dot_wide_rhs(o, seg) * (1.0 / N)
    d = o - mean
    var = _dot_wide_rhs(d * d, seg) * (1.0 / N)
    y_ref[0] = d * lax.rsqrt(var + RWKV_GN_EPS) * lng_ref[...] + lnb_ref[...] + bonus


def _rwkv(z, mu, w0, a0, k_k, k_a, r_k, ln_g, ln_b, w2a2, cb):
    B, S, _ = z.shape
    ns = S // cb
    seg = jnp.kron(jnp.eye(RWKV_HEADS, dtype=F32), jnp.ones((HEAD_DIM, HEAD_DIM), F32)).astype(BF16)
    ltri = jnp.kron(jnp.eye(cb // CHUNK, dtype=F32), jnp.tril(jnp.ones((CHUNK, CHUNK), F32))).astype(BF16)
    full = lambda a: pl.BlockSpec(a.shape, lambda b, s: (0,) * a.ndim)
    args = (mu, w0, a0, k_k, k_a, r_k, ln_g, ln_b, w2a2, seg, ltri)
    return pl.pallas_call(
        _rwkv_kernel,
        grid=(B, ns),
        in_specs=[pl.BlockSpec((1, cb, RWKV_SHIFT_W), lambda b, s: (b, s, 0))] + [full(a) for a in args],
        out_specs=pl.BlockSpec((1, cb, RWKV_W), lambda b, s: (b, s, 0)),
        out_shape=jax.ShapeDtypeStruct((B, S, RWKV_W), F32),
        scratch_shapes=[pltpu.VMEM((RWKV_W, RWKV_W), F32),
                        pltpu.VMEM((1, RWKV_SHIFT_W), F32)],
        compiler_params=_cparams(("arbitrary", "arbitrary")),
        name="rwkv7",
    )(z, *args)


def _out_kernel(x_ref, mod_ref, yf_ref, yr_ref, p_ref, g_ref, pw_ref, ps_ref, wout_ref, npost_ref,
                o_ref, halo_ref):
    si = pl.program_id(1)

    @pl.when(si == 0)
    def _():
        halo_ref[...] = jnp.zeros_like(halo_ref)

    D = D_MODEL
    p = p_ref[0]
    ts = p.shape[0]
    e = jnp.concatenate([halo_ref[...], p], axis=0)
    halo_ref[...] = p[ts - POOL_HALO:ts, :]
    sums = []
    s_run = e
    shift = 1
    for _w in POOL_WINDOWS:
        s_run = s_run + pltpu.roll(s_run, shift, 0)
        sums.append(s_run[POOL_HALO:, :])
        shift *= 2
    pos = si * ts + lax.broadcasted_iota(jnp.int32, (ts, 1), 0) + 1
    lane = lax.broadcasted_iota(jnp.int32, (ts, POOL_W), 1)
    pooled = jnp.zeros((ts, POOL_W), F32)
    for gi, w in enumerate(POOL_WINDOWS):
        cnt = jnp.minimum(pos, w).astype(F32)
        grp = (lane >= POOL_GROUP_DIM * gi) & (lane < POOL_GROUP_DIM * (gi + 1))
        pooled = jnp.where(grp, sums[gi] / cnt, pooled)
    pooled = pooled - p
    y_pool = _dot(pooled.astype(BF16), pw_ref[...]) * ps_ref[...]

    y = jnp.concatenate([yf_ref[0].astype(F32), yr_ref[0], y_pool], axis=-1) * _silu(g_ref[0].astype(F32))
    t = _dot(y.astype(BF16), wout_ref[...])
    ms = jnp.mean(t * t, axis=-1, keepdims=True)
    tn = t * lax.rsqrt(ms + NORM_EPS) * npost_ref[...]
    gate = mod_ref[0][:, 2 * D:3 * D]
    o_ref[0] = x_ref[0] + gate * tn


def _out_proj(x, mod_l, y_fox, y_rwkv, p, g, pw_bd, pool_scale, w_out, npost, ts):
    B, S, D = x.shape
    ns = S // ts
    full = lambda a: pl.BlockSpec(a.shape, lambda b, s: (0,) * a.ndim)
    seq_spec = lambda w: pl.BlockSpec((1, ts, w), lambda b, s: (b, s, 0))
    return pl.pallas_call(
        _out_kernel,
        grid=(B, ns),
        in_specs=[seq_spec(D), pl.BlockSpec((1, 1, 3 * D), lambda b, s: (b, 0, 0)),
                  seq_spec(FOX_W), seq_spec(RWKV_W), seq_spec(POOL_W), seq_spec(D_MIX),
                  full(pw_bd), full(pool_scale), full(w_out), full(npost)],
        out_specs=seq_spec(D),
        out_shape=jax.ShapeDtypeStruct((B, S, D), F32),
        scratch_shapes=[pltpu.VMEM((POOL_HALO, POOL_W), F32)],
        compiler_params=_cparams(("arbitrary", "arbitrary")),
        name="out_proj",
    )(x, mod_l, y_fox, y_rwkv, p, g, pw_bd, pool_scale, w_out, npost)


def _block_diag(blocks):
    n = blocks.shape[0]
    d = blocks.shape[1]
    out = jnp.zeros((n * d, n * d), blocks.dtype)
    for i in range(n):
        out = out.at[i * d:(i + 1) * d, i * d:(i + 1) * d].set(blocks[i])
    return out


def kernel(x, c, ada_w, ada_b, norm_pre, norm_post, w_in, fox_q_gain, fox_k_gain, fox_f_bias, rwkv_mu, rwkv_w0, rwkv_w2, rwkv_a0, rwkv_a2, rwkv_k_k, rwkv_k_a, rwkv_r_k, rwkv_ln_g, rwkv_ln_b, pool_w, pool_scale, w_out):
    B, S, D = x.shape
    L = ada_w.shape[0]
    ts = min(512, S)
    tk = min(512, S // 2)
    cb = min(256, S)

    mod = _adaln_mod(c, ada_w, ada_b)
    o_q, o_k, o_v, o_f, o_z, o_p = (0, FOX_W, 2 * FOX_W, 3 * FOX_W, 3 * FOX_W + FOX_HEADS,
                                    3 * FOX_W + FOX_HEADS + RWKV_SHIFT_W)
    o_g = o_p + POOL_W
    row = lambda a: a.reshape(1, -1)

    for l in range(L):
        w = w_in[l]
        wq = w[:, o_q:o_k].astype(BF16)
        wk = w[:, o_k:o_v].astype(BF16)
        wv = w[:, o_v:o_f].astype(BF16)
        wf = jnp.pad(w[:, o_f:o_z], ((0, 0), (0, F_PAD - FOX_HEADS))).astype(BF16)
        wz = w[:, o_z:o_p].astype(BF16)
        wp = w[:, o_p:o_g].astype(BF16)
        wg = w[:, o_g:].astype(BF16)
        fbias = jnp.pad(fox_f_bias[l], (0, F_PAD - FOX_HEADS)).reshape(1, F_PAD)
        qg2 = jnp.tile(fox_q_gain[l], 2).reshape(1, LANES)
        kg2 = jnp.tile(fox_k_gain[l], 2).reshape(1, LANES)
        mod_l = mod[l].reshape(B, 1, 3 * D)

        q_aug, k_aug, v_aug, z, p, g = _in_proj(x, mod_l, row(norm_pre[l]), wq, wk, wv, wf, wz, wp, wg,
                                                fbias, qg2, kg2, ts)
        y_fox = _fox_attention(q_aug, k_aug, v_aug, tk)

        w2a2 = jnp.zeros((2 * HEAD_DIM, 2 * RWKV_W), F32)
        w2a2 = w2a2.at[:RWKV_DECAY_RANK, :RWKV_W].set(rwkv_w2[l])
        w2a2 = w2a2.at[RWKV_DECAY_RANK:, RWKV_W:].set(rwkv_a2[l])
        y_rwkv = _rwkv(z, row(rwkv_mu[l]), row(rwkv_w0[l]), row(rwkv_a0[l]), row(rwkv_k_k[l]),
                       row(rwkv_k_a[l]), row(rwkv_r_k[l]), row(rwkv_ln_g[l]), row(rwkv_ln_b[l]),
                       w2a2, cb)

        pw_bd = _block_diag(pool_w[l]).astype(BF16)
        x = _out_proj(x, mod_l, y_fox, y_rwkv, p, g, pw_bd, row(pool_scale[l]),
                      w_out[l].astype(BF16), row(norm_post[l]), ts)
    return x
```

```python
import functools
import math

import jax
import jax.numpy as jnp
from jax import lax
from jax.experimental import pallas as pl
from jax.experimental.pallas import tpu as pltpu

F32 = jnp.float32
BF16 = jnp.bfloat16

D_MODEL = 1024
HEAD_DIM = 64
FOX_HEADS = 8
FOX_W = FOX_HEADS * HEAD_DIM
RWKV_HEADS = 4
RWKV_W = RWKV_HEADS * HEAD_DIM
POOL_GROUPS = 4
POOL_GROUP_DIM = 64
POOL_W = POOL_GROUPS * POOL_GROUP_DIM
POOL_WINDOWS = (2, 4, 8, 16)
POOL_HALO = 16
D_MIX = FOX_W + RWKV_W + POOL_W
RWKV_DECAY_RANK = 64
RWKV_ICL_RANK = 64
RWKV_SHIFT_W = 3 * RWKV_W + RWKV_DECAY_RANK + RWKV_ICL_RANK
NORM_EPS = 1e-6
RWKV_GN_EPS = 64e-5
DECAY_SCALE = float(math.exp(-0.5))
LOG2E = float(1.0 / math.log(2.0))

LANES = 128
F_PAD = LANES
CHUNK = 64
NEG_BIG = -1e30

VMEM_LIMIT = 56 * 1024 * 1024


def _cparams(sem):
    return pltpu.CompilerParams(dimension_semantics=sem, vmem_limit_bytes=VMEM_LIMIT)


def _dot(a, b):
    return jnp.dot(a, b, preferred_element_type=F32)


def _dot_nt(a, b):
    return lax.dot_general(a, b, (((1,), (1,)), ((), ())), preferred_element_type=F32)


def _dot_tn(a, b):
    return lax.dot_general(a, b, (((0,), (0,)), ((), ())), preferred_element_type=F32)


def _split2(x):
    hi = x.astype(BF16)
    lo = (x - hi.astype(F32)).astype(BF16)
    return hi, lo


def _split3(x):
    hi = x.astype(BF16)
    r = x - hi.astype(F32)
    mid = r.astype(BF16)
    lo = (r - mid.astype(F32)).astype(BF16)
    return hi, mid, lo


def _dot_exact_lhs(l_bf16, x):
    h, m, l = _split3(x)
    return _dot(l_bf16, h) + _dot(l_bf16, m) + _dot(l_bf16, l)


def _dot_wide_lhs(l_bf16, x):
    h, l = _split2(x)
    return _dot(l_bf16, h) + _dot(l_bf16, l)


def _dot_wide_rhs(x, r_bf16):
    h, l = _split2(x)
    return _dot(h, r_bf16) + _dot(l, r_bf16)


def _dot3(a, b, dot=_dot):
    ah, al = _split2(a)
    bh, bl = _split2(b)
    return dot(ah, bh) + dot(ah, bl) + dot(al, bh)


def _sigmoid(x):
    return 1.0 / (1.0 + jnp.exp(-x))


def _silu(x):
    return x * _sigmoid(x)


def _mod_kernel(c_ref, w_ref, b_ref, o_ref):
    sc = _silu(c_ref[...])
    o_ref[0] = jnp.dot(sc, w_ref[0], precision=lax.Precision.HIGHEST,
                       preferred_element_type=F32) + b_ref[0]


def _adaln_mod(c, ada_w, ada_b):
    L, D, D3 = ada_w.shape
    B = c.shape[0]
    nj = D3 // D
    return pl.pallas_call(
        _mod_kernel,
        grid=(L, nj),
        in_specs=[
            pl.BlockSpec((B, D), lambda l, j: (0, 0)),
            pl.BlockSpec((1, D, D), lambda l, j: (l, 0, j)),
            pl.BlockSpec((1, 1, D), lambda l, j: (l, 0, j)),
        ],
        out_specs=pl.BlockSpec((1, B, D), lambda l, j: (l, 0, j)),
        out_shape=jax.ShapeDtypeStruct((L, B, D3), F32),
        compiler_params=_cparams(("arbitrary", "arbitrary")),
        name="adaln_mod",
    )(c, ada_w, ada_b.reshape(L, 1, D3))


def _in_kernel(x_ref, mod_ref, npre_ref, wq_ref, wk_ref, wv_ref, wf_ref, wz_ref, wp_ref, wg_ref,
               fb_ref, qg_ref, kg_ref, ltri_ref,
               q_out, k_out, v_out, z_out, p_out, g_out, carry_ref):
    si = pl.program_id(1)

    @pl.when(si == 0)
    def _():
        carry_ref[...] = jnp.zeros_like(carry_ref)

    D = D_MODEL
    x = x_ref[0]
    ts = x.shape[0]
    ms = jnp.mean(x * x, axis=-1, keepdims=True)
    xn = x * lax.rsqrt(ms + NORM_EPS) * npre_ref[...]
    mod = mod_ref[0]
    shift = mod[:, 0:D]
    scale = mod[:, D:2 * D]
    h = (xn * (1.0 + scale) + shift).astype(BF16)

    f = _dot(h, wf_ref[...]) + fb_ref[...]
    logf = (jnp.minimum(f, 0.0) - jnp.log1p(jnp.exp(-jnp.abs(f)))) * LOG2E
    cum = _dot_exact_lhs(ltri_ref[...], logf) + carry_ref[...]
    carry_ref[...] = cum[ts - 1:ts, :]
    c1 = cum.astype(BF16).astype(F32)
    r1 = cum - c1
    c2 = r1.astype(BF16).astype(F32)
    c3 = r1 - c2

    uq = _dot(h, wq_ref[...])
    uk = _dot(h, wk_ref[...])
    uv = _dot(h, wv_ref[...])

    lane = lax.broadcasted_iota(jnp.int32, (ts, LANES), 1)
    lo = lane < HEAD_DIM
    qscale = (HEAD_DIM ** -0.5) * LOG2E

    def head_norm(x2, gain2):
        sq = x2 * x2
        ss_lo = jnp.sum(jnp.where(lo, sq, 0.0), axis=-1, keepdims=True)
        ss_hi = jnp.sum(jnp.where(lo, 0.0, sq), axis=-1, keepdims=True)
        inv = jnp.where(lo, lax.rsqrt(ss_lo * (1.0 / HEAD_DIM) + NORM_EPS),
                        lax.rsqrt(ss_hi * (1.0 / HEAD_DIM) + NORM_EPS))
        return x2 * inv * gain2

    def extras(e0, hcol, for_q):
        a1 = c1[:, hcol:hcol + 1]
        a2 = c2[:, hcol:hcol + 1]
        a3 = c3[:, hcol:hcol + 1]
        one = jnp.ones((ts, 1), F32)
        if for_q:
            vals = (a1, a2, a3, one, one, one)
        else:
            vals = (one, one, one, -a1, -a2, -a3)
        e = jnp.zeros((ts, LANES), F32)
        for i, val in enumerate(vals):
            e = jnp.where(lane == e0 + i, val, e)
        return e

    for j in range(FOX_HEADS // 2):
        sl = slice(LANES * j, LANES * (j + 1))
        qn = head_norm(uq[:, sl], qg_ref[...]) * qscale
        kn = head_norm(uk[:, sl], kg_ref[...])
        v2 = uv[:, sl]
        he, ho = 2 * j, 2 * j + 1
        q_out[0, he] = jnp.where(lo, qn, extras(HEAD_DIM, he, True)).astype(BF16)
        k_out[0, he] = jnp.where(lo, kn, extras(HEAD_DIM, he, False)).astype(BF16)
        v_out[0, he] = jnp.where(lo, v2, jnp.where(lane == HEAD_DIM, 1.0, 0.0)).astype(BF16)
        q_out[0, ho] = jnp.where(lo, extras(0, ho, True), qn).astype(BF16)
        k_out[0, ho] = jnp.where(lo, extras(0, ho, False), kn).astype(BF16)
        v_out[0, ho] = jnp.where(lo, jnp.where(lane == 0, 1.0, 0.0), v2).astype(BF16)

    z_out[0] = _dot(h, wz_ref[...])
    p_out[0] = _dot(h, wp_ref[...])
    g_out[0] = _dot(h, wg_ref[...]).astype(BF16)


def _in_proj(x, mod_l, npre, wq, wk, wv, wf, wz, wp, wg, fbias, qg2, kg2, ts):
    B, S, D = x.shape
    ns = S // ts
    ltri = jnp.tril(jnp.ones((ts, ts), F32)).astype(BF16)
    full = lambda a: pl.BlockSpec(a.shape, lambda b, s: (0,) * a.ndim)
    hs = jax.ShapeDtypeStruct((B, FOX_HEADS, S, LANES), BF16)
    head_spec = pl.BlockSpec((1, FOX_HEADS, ts, LANES), lambda b, s: (b, 0, s, 0))
    seq_spec = lambda w: pl.BlockSpec((1, ts, w), lambda b, s: (b, s, 0))
    return pl.pallas_call(
        _in_kernel,
        grid=(B, ns),
        in_specs=[
            seq_spec(D),
            pl.BlockSpec((1, 1, 3 * D), lambda b, s: (b, 0, 0)),
            full(npre), full(wq), full(wk), full(wv), full(wf), full(wz), full(wp), full(wg),
            full(fbias), full(qg2), full(kg2), full(ltri),
        ],
        out_specs=[head_spec, head_spec, head_spec,
                   seq_spec(RWKV_SHIFT_W), seq_spec(POOL_W), seq_spec(D_MIX)],
        out_shape=[hs, hs, hs,
                   jax.ShapeDtypeStruct((B, S, RWKV_SHIFT_W), F32),
                   jax.ShapeDtypeStruct((B, S, POOL_W), F32),
                   jax.ShapeDtypeStruct((B, S, D_MIX), BF16)],
        scratch_shapes=[pltpu.VMEM((1, LANES), F32)],
        compiler_params=_cparams(("arbitrary", "arbitrary")),
        name="in_proj",
    )(x, mod_l, npre, wq, wk, wv, wf, wz, wp, wg, fbias, qg2, kg2, ltri)


def _attn_kernel(q_ref, k_ref, v_ref, o_ref, *, tk):
    qi = pl.program_id(2)
    tq = 2 * tk
    row = lax.broadcasted_iota(jnp.int32, (tq, tk), 0)
    col = lax.broadcasted_iota(jnp.int32, (tq, tk), 1)
    causal = col <= row
    lane = lax.broadcasted_iota(jnp.int32, (tq, LANES), 1)
    lo = lane < HEAD_DIM

    heads = range(2)

    def scores(hh, q, blk, w=tk):
        off = pl.multiple_of(blk * w, w)
        return _dot_nt(q, k_ref[0, hh, pl.ds(off, w), :])

    def update(hh, s, blk, m, acc, w=tk):
        off = pl.multiple_of(blk * w, w)
        m_new = jnp.maximum(m, jnp.max(s, axis=-1, keepdims=True))
        alpha = jnp.exp2(m - m_new)
        p = jnp.exp2(s - m_new)
        acc = alpha * acc + _dot(p.astype(BF16), v_ref[0, hh, pl.ds(off, w), :])
        return m_new, acc

    m0 = jnp.full((tq, 1), NEG_BIG, F32)
    a0 = jnp.zeros((tq, LANES), F32)

    def body(j, carry):
        ms, accs_ = carry
        ss = [scores(hh, q_ref[0, hh], j, tq) for hh in heads]
        upd = [update(hh, ss[hh], j, ms[hh], accs_[hh], tq) for hh in heads]
        return tuple(u[0] for u in upd), tuple(u[1] for u in upd)

    ms, accs = lax.fori_loop(0, qi, body, ((m0, m0), (a0, a0)))

    row_b = lax.broadcasted_iota(jnp.int32, (tk, tq), 0)
    col_b = lax.broadcasted_iota(jnp.int32, (tk, tq), 1)
    causal_b = col_b <= row_b + tk
    s_up = [jnp.where(causal[0:tk], scores(hh, q_ref[0, hh, 0:tk, :], 2 * qi), NEG_BIG) for hh in heads]
    s_lo = [jnp.where(causal_b, scores(hh, q_ref[0, hh, tk:tq, :], qi, tq), NEG_BIG) for hh in heads]
    up = [update(hh, s_up[hh], 2 * qi, ms[hh][0:tk], accs[hh][0:tk])[1] for hh in heads]
    low = [update(hh, s_lo[hh], qi, ms[hh][tk:tq], accs[hh][tk:tq], tq)[1] for hh in heads]
    accs = [jnp.concatenate([up[hh], low[hh]], axis=0) for hh in heads]

    l_e = accs[0][:, HEAD_DIM:HEAD_DIM + 1]
    l_o = accs[1][:, 0:1]
    o_ref[0] = jnp.where(lo, accs[0] / l_e, accs[1] / l_o).astype(o_ref.dtype)


def _fox_attention(q, k, v, tk):
    B, H, S, _ = q.shape
    tq = 2 * tk
    nq = S // tq
    return pl.pallas_call(
        functools.partial(_attn_kernel, tk=tk),
        grid=(B, H // 2, nq),
        in_specs=[
            pl.BlockSpec((1, 2, tq, LANES), lambda b, p, i: (b, p, i, 0)),
            pl.BlockSpec((1, 2, S, LANES), lambda b, p, i: (b, p, 0, 0)),
            pl.BlockSpec((1, 2, S, LANES), lambda b, p, i: (b, p, 0, 0)),
        ],
        out_specs=pl.BlockSpec((1, tq, LANES), lambda b, p, i: (b, i, p)),
        out_shape=jax.ShapeDtypeStruct((B, S, FOX_W), BF16),
        compiler_params=_cparams(("arbitrary", "arbitrary", "arbitrary")),
        name="fox_attention",
    )(q, k, v)


def _mm(a, b):
    return _dot(a.astype(BF16), b.astype(BF16))


def _mm_nt(a, b):
    return _dot_nt(a.astype(BF16), b.astype(BF16))


def _mm_tn(a, b):
    return _dot_tn(a.astype(BF16), b.astype(BF16))


def _unit_lower_inverses(ms, row, col):
    eye = (row == col).astype(F32)
    same = lambda sh: (row >> sh) == (col >> sh)
    ts = [eye + jnp.where(same(1), m, 0.0) for m in ms]
    for sh in range(1, 6):
        band = jnp.logical_and(same(sh + 1), jnp.logical_not(same(sh)))
        ets = [_mm(jnp.where(band, m, 0.0), t) for m, t in zip(ms, ts)]
        ts = [t + _mm(t, et) for t, et in zip(ts, ets)]
    return ts


def _rwkv_kernel(z_ref, mu_ref, w0_ref, a0_ref, kk_ref, ka_ref, rk_ref, lng_ref, lnb_ref,
                 w2a2_ref, seg_ref, ltri_ref, y_ref, state_ref, prev_ref):
    si = pl.program_id(1)

    @pl.when(si == 0)
    def _():
        state_ref[...] = jnp.zeros_like(state_ref)
        prev_ref[...] = jnp.zeros_like(prev_ref)

    W = RWKV_W
    N = HEAD_DIM
    z = z_ref[0]
    cb = z.shape[0]
    row = lax.broadcasted_iota(jnp.int32, (cb, 1), 0)
    zprev = jnp.where(row == 0, prev_ref[...], pltpu.roll(z, 1, 0))
    prev_ref[...] = z[cb - 1:cb, :]
    zf = z + (zprev - z) * mu_ref[...]
    r = zf[:, 0:W]
    k = zf[:, W:2 * W]
    v = zf[:, 2 * W:3 * W]
    t6 = zf[:, 3 * W:3 * W + 2 * N]
    lane = lax.broadcasted_iota(jnp.int32, (cb, 2 * N), 1)
    tz = jnp.where(lane < N, jnp.tanh(t6), t6)
    dwa = _dot3(tz, w2a2_ref[...])
    logw = -DECAY_SCALE * _sigmoid(w0_ref[...] + dwa[:, 0:W])
    a = _sigmoid(a0_ref[...] + dwa[:, W:2 * W])

    seg = seg_ref[...]
    kkr = k * kk_ref[...]
    kk = kkr / jnp.maximum(jnp.sqrt(_dot_wide_rhs(kkr * kkr, seg)), 1e-12)
    k2 = k * (1.0 + (a - 1.0) * ka_ref[...])
    bonus = _dot_wide_rhs(r * k2 * rk_ref[...], seg) * v

    logg = _dot_exact_lhs(ltri_ref[...], logw)
    g = jnp.exp(logg)
    gex = jnp.exp(logg - logw)
    gi = jnp.exp(-logg)
    at = -kk * gex
    rt = r * g
    bt = kk * a * gi
    kt = k2 * gi

    rw = lax.broadcasted_iota(jnp.int32, (W, W), 0)
    cw = lax.broadcasted_iota(jnp.int32, (W, W), 1)
    head_blk = (rw >> 6) == (cw >> 6)
    tw = lax.broadcasted_iota(jnp.int32, (CHUNK, W), 0)
    sw = lax.broadcasted_iota(jnp.int32, (CHUNK, W), 1) & (CHUNK - 1)
    strict_w = sw < tw
    incl_w = sw <= tw

    def tall_f32(zc):
        return jnp.where(head_blk, jnp.concatenate([zc] * RWKV_HEADS, axis=0), 0.0)

    def tall(zc):
        return tall_f32(zc).astype(BF16)

    def fold(y):
        return y[0:CHUNK] + y[CHUNK:2 * CHUNK] + y[2 * CHUNK:3 * CHUNK] + y[3 * CHUNK:4 * CHUNK]

    nchunk = cb // CHUNK
    chunks = range(nchunk)
    rows = [slice(CHUNK * c, CHUNK * (c + 1)) for c in chunks]
    glast = [g[CHUNK * (c + 1) - 1:CHUNK * (c + 1), :] for c in chunks]
    t_a = [tall(at[rs]) for rs in rows]
    t_bk = [jnp.concatenate([tall(bt[rs]), tall(kt[rs])], axis=0) for rs in rows]
    t_v = [tall(v[rs]) for rs in rows]
    pp = [_dot_nt(jnp.concatenate([at[rows[c]], rt[rows[c]]], axis=0).astype(BF16), t_bk[c]) for c in chunks]
    m_ab = [tall_f32(jnp.where(strict_w, p[0:CHUNK, 0:W], 0.0)) for p in pp]
    m_akrk = [jnp.concatenate([jnp.where(strict_w, p[0:CHUNK, W:2 * W], 0.0),
                               jnp.where(incl_w, p[CHUNK:2 * CHUNK, W:2 * W], 0.0)], axis=0) for p in pp]
    m_rb = [jnp.where(incl_w, p[CHUNK:2 * CHUNK, 0:W], 0.0).astype(BF16) for p in pp]
    xo = [_mm(m_akrk[c], t_v[c]) for c in chunks]
    d0 = [jnp.where(head_blk, _mm_tn(v[rows[c]], kt[rows[c]] * glast[c]), 0.0) for c in chunks]
    bh = [(bt[rows[c]] * glast[c]).astype(BF16) for c in chunks]
    tinv = _unit_lower_inverses(m_ab, rw, cw)
    wu = [_mm(fold(tinv[c]), jnp.concatenate([t_a[c], tall(xo[c][0:CHUNK])], axis=1)) for c in chunks]
    w_all = [x[:, 0:W] for x in wu]
    u0_all = [x[:, W:2 * W] for x in wu]
    o0_all = [x[CHUNK:2 * CHUNK] for x in xo]
    pq = [_dot_tn(wu[c].astype(BF16), bh[c]) for c in chunks]
    phi = [jnp.where(head_blk, x[0:W], 0.0).astype(BF16) for x in pq]
    psi = [jnp.where(head_blk, pq[c][W:2 * W], 0.0) + d0[c] for c in chunks]
    mw = [_dot(m_rb[c], jnp.concatenate([tall(w_all[c]), tall(u0_all[c])], axis=1)) for c in chunks]
    omega = [(rt[rows[c]] + mw[c][:, 0:W]).astype(BF16) for c in chunks]
    xi = [mw[c][:, W:2 * W] + o0_all[c] for c in chunks]

    outs = []
    s_bd = state_ref[...]
    for c in chunks:
        s_bf = s_bd.astype(BF16)
        outs.append(_dot_nt(omega[c], s_bf) + xi[c])
        s_bd = s_bd * glast[c] + _dot(s_bf, phi[c]) + psi[c]
    state_ref[...] = s_bd

    o = jnp.concatenate(outs, axis=0)
    mean = _dot_wide_rhs(o, seg) * (1.0 / N)
    d = o - mean
    var = _dot_wide_rhs(d * d, seg) * (1.0 / N)
    y_ref[0] = d * lax.rsqrt(var + RWKV_GN_EPS) * lng_ref[...] + lnb_ref[...] + bonus


def _rwkv(z, mu, w0, a0, k_k, k_a, r_k, ln_g, ln_b, w2a2, cb):
    B, S, _ = z.shape
    ns = S // cb
    seg = jnp.kron(jnp.eye(RWKV_HEADS, dtype=F32), jnp.ones((HEAD_DIM, HEAD_DIM), F32)).astype(BF16)
    ltri = jnp.kron(jnp.eye(cb // CHUNK, dtype=F32), jnp.tril(jnp.ones((CHUNK, CHUNK), F32))).astype(BF16)
    full = lambda a: pl.BlockSpec(a.shape, lambda b, s: (0,) * a.ndim)
    args = (mu, w0, a0, k_k, k_a, r_k, ln_g, ln_b, w2a2, seg, ltri)
    return pl.pallas_call(
        _rwkv_kernel,
        grid=(B, ns),
        in_specs=[pl.BlockSpec((1, cb, RWKV_SHIFT_W), lambda b, s: (b, s, 0))] + [full(a) for a in args],
        out_specs=pl.BlockSpec((1, cb, RWKV_W), lambda b, s: (b, s, 0)),
        out_shape=jax.ShapeDtypeStruct((B, S, RWKV_W), F32),
        scratch_shapes=[pltpu.VMEM((RWKV_W, RWKV_W), F32),
                        pltpu.VMEM((1, RWKV_SHIFT_W), F32)],
        compiler_params=_cparams(("arbitrary", "arbitrary")),
        name="rwkv7",
    )(z, *args)


def _out_kernel(x_ref, mod_ref, yf_ref, yr_ref, p_ref, g_ref, pw_ref, ps_ref, wout_ref, npost_ref,
                o_ref, halo_ref):
    si = pl.program_id(1)

    @pl.when(si == 0)
    def _():
        halo_ref[...] = jnp.zeros_like(halo_ref)

    D = D_MODEL
    p = p_ref[0]
    ts = p.shape[0]
    e = jnp.concatenate([halo_ref[...], p], axis=0)
    halo_ref[...] = p[ts - POOL_HALO:ts, :]
    sums = []
    s_run = e
    shift = 1
    for _w in POOL_WINDOWS:
        s_run = s_run + pltpu.roll(s_run, shift, 0)
        sums.append(s_run[POOL_HALO:, :])
        shift *= 2
    pos = si * ts + lax.broadcasted_iota(jnp.int32, (ts, 1), 0) + 1
    lane = lax.broadcasted_iota(jnp.int32, (ts, POOL_W), 1)
    pooled = jnp.zeros((ts, POOL_W), F32)
    for gi, w in enumerate(POOL_WINDOWS):
        cnt = jnp.minimum(pos, w).astype(F32)
        grp = (lane >= POOL_GROUP_DIM * gi) & (lane < POOL_GROUP_DIM * (gi + 1))
        pooled = jnp.where(grp, sums[gi] / cnt, pooled)
    pooled = pooled - p
    y_pool = _dot(pooled.astype(BF16), pw_ref[...]) * ps_ref[...]

    y = jnp.concatenate([yf_ref[0].astype(F32), yr_ref[0], y_pool], axis=-1) * _silu(g_ref[0].astype(F32))
    t = _dot(y.astype(BF16), wout_ref[...])
    ms = jnp.mean(t * t, axis=-1, keepdims=True)
    tn = t * lax.rsqrt(ms + NORM_EPS) * npost_ref[...]
    gate = mod_ref[0][:, 2 * D:3 * D]
    o_ref[0] = x_ref[0] + gate * tn


def _out_proj(x, mod_l, y_fox, y_rwkv, p, g, pw_bd, pool_scale, w_out, npost, ts):
    B, S, D = x.shape
    ns = S // ts
    full = lambda a: pl.BlockSpec(a.shape, lambda b, s: (0,) * a.ndim)
    seq_spec = lambda w: pl.BlockSpec((1, ts, w), lambda b, s: (b, s, 0))
    return pl.pallas_call(
        _out_kernel,
        grid=(B, ns),
        in_specs=[seq_spec(D), pl.BlockSpec((1, 1, 3 * D), lambda b, s: (b, 0, 0)),
                  seq_spec(FOX_W), seq_spec(RWKV_W), seq_spec(POOL_W), seq_spec(D_MIX),
                  full(pw_bd), full(pool_scale), full(w_out), full(npost)],
        out_specs=seq_spec(D),
        out_shape=jax.ShapeDtypeStruct((B, S, D), F32),
        scratch_shapes=[pltpu.VMEM((POOL_HALO, POOL_W), F32)],
        compiler_params=_cparams(("arbitrary", "arbitrary")),
        name="out_proj",
    )(x, mod_l, y_fox, y_rwkv, p, g, pw_bd, pool_scale, w_out, npost)


def _block_diag(blocks):
    n = blocks.shape[0]
    d = blocks.shape[1]
    out = jnp.zeros((n * d, n * d), blocks.dtype)
    for i in range(n):
        out = out.at[i * d:(i + 1) * d, i * d:(i + 1) * d].set(blocks[i])
    return out


def kernel(x, c, ada_w, ada_b, norm_pre, norm_post, w_in, fox_q_gain, fox_k_gain, fox_f_bias, rwkv_mu, rwkv_w0, rwkv_w2, rwkv_a0, rwkv_a2, rwkv_k_k, rwkv_k_a, rwkv_r_k, rwkv_ln_g, rwkv_ln_b, pool_w, pool_scale, w_out):
    B, S, D = x.shape
    L = ada_w.shape[0]
    ts = min(512, S)
    tk = min(512, S // 2)
    cb = min(256, S)

    mod = _adaln_mod(c, ada_w, ada_b)
    o_q, o_k, o_v, o_f, o_z, o_p = (0, FOX_W, 2 * FOX_W, 3 * FOX_W, 3 * FOX_W + FOX_HEADS,
                                    3 * FOX_W + FOX_HEADS + RWKV_SHIFT_W)
    o_g = o_p + POOL_W
    row = lambda a: a.reshape(1, -1)

    for l in range(L):
        w = w_in[l]
        wq = w[:, o_q:o_k].astype(BF16)
        wk = w[:, o_k:o_v].astype(BF16)
        wv = w[:, o_v:o_f].astype(BF16)
        wf = jnp.pad(w[:, o_f:o_z], ((0, 0), (0, F_PAD - FOX_HEADS))).astype(BF16)
        wz = w[:, o_z:o_p].astype(BF16)
        wp = w[:, o_p:o_g].astype(BF16)
        wg = w[:, o_g:].astype(BF16)
        fbias = jnp.pad(fox_f_bias[l], (0, F_PAD - FOX_HEADS)).reshape(1, F_PAD)
        qg2 = jnp.tile(fox_q_gain[l], 2).reshape(1, LANES)
        kg2 = jnp.tile(fox_k_gain[l], 2).reshape(1, LANES)
        mod_l = mod[l].reshape(B, 1, 3 * D)

        q_aug, k_aug, v_aug, z, p, g = _in_proj(x, mod_l, row(norm_pre[l]), wq, wk, wv, wf, wz, wp, wg,
                                                fbias, qg2, kg2, ts)
        y_fox = _fox_attention(q_aug, k_aug, v_aug, tk)

        w2a2 = jnp.zeros((2 * HEAD_DIM, 2 * RWKV_W), F32)
        w2a2 = w2a2.at[:RWKV_DECAY_RANK, :RWKV_W].set(rwkv_w2[l])
        w2a2 = w2a2.at[RWKV_DECAY_RANK:, RWKV_W:].set(rwkv_a2[l])
        y_rwkv = _rwkv(z, row(rwkv_mu[l]), row(rwkv_w0[l]), row(rwkv_a0[l]), row(rwkv_k_k[l]),
                       row(rwkv_k_a[l]), row(rwkv_r_k[l]), row(rwkv_ln_g[l]), row(rwkv_ln_b[l]),
                       w2a2, cb)

        pw_bd = _block_diag(pool_w[l]).astype(BF16)
        x = _out_proj(x, mod_l, y_fox, y_rwkv, p, g, pw_bd, row(pool_scale[l]),
                      w_out[l].astype(BF16), row(norm_post[l]), ts)
    return x
```

```python
import functools
import math

import jax
import jax.numpy as jnp
from jax import lax
from jax.experimental import pallas as pl
from jax.experimental.pallas import tpu as pltpu

F32 = jnp.float32
BF16 = jnp.bfloat16

D_MODEL = 1024
HEAD_DIM = 64
FOX_HEADS = 8
FOX_W = FOX_HEADS * HEAD_DIM
RWKV_HEADS = 4
RWKV_W = RWKV_HEADS * HEAD_DIM
POOL_GROUPS = 4
POOL_GROUP_DIM = 64
POOL_W = POOL_GROUPS * POOL_GROUP_DIM
POOL_WINDOWS = (2, 4, 8, 16)
POOL_HALO = 16
D_MIX = FOX_W + RWKV_W + POOL_W
RWKV_DECAY_RANK = 64
RWKV_ICL_RANK = 64
RWKV_SHIFT_W = 3 * RWKV_W + RWKV_DECAY_RANK + RWKV_ICL_RANK
NORM_EPS = 1e-6
RWKV_GN_EPS = 64e-5
DECAY_SCALE = float(math.exp(-0.5))
LOG2E = float(1.0 / math.log(2.0))

LANES = 128
F_PAD = LANES
CHUNK = 64
NEG_BIG = -1e30

VMEM_LIMIT = 56 * 1024 * 1024


def _cparams(sem):
    return pltpu.CompilerParams(dimension_semantics=sem, vmem_limit_bytes=VMEM_LIMIT)


def _dot(a, b):
    return jnp.dot(a, b, preferred_element_type=F32)


def _dot_nt(a, b):
    return lax.dot_general(a, b, (((1,), (1,)), ((), ())), preferred_element_type=F32)


def _dot_tn(a, b):
    return lax.dot_general(a, b, (((0,), (0,)), ((), ())), preferred_element_type=F32)


def _split2(x):
    hi = x.astype(BF16)
    lo = (x - hi.astype(F32)).astype(BF16)
    return hi, lo


def _split3(x):
    hi = x.astype(BF16)
    r = x - hi.astype(F32)
    mid = r.astype(BF16)
    lo = (r - mid.astype(F32)).astype(BF16)
    return hi, mid, lo


def _dot_exact_lhs(l_bf16, x):
    h, m, l = _split3(x)
    return _dot(l_bf16, h) + _dot(l_bf16, m) + _dot(l_bf16, l)


def _dot_wide_lhs(l_bf16, x):
    h, l = _split2(x)
    return _dot(l_bf16, h) + _dot(l_bf16, l)


def _dot_wide_rhs(x, r_bf16):
    h, l = _split2(x)
    return _dot(h, r_bf16) + _dot(l, r_bf16)


def _dot3(a, b, dot=_dot):
    ah, al = _split2(a)
    bh, bl = _split2(b)
    return dot(ah, bh) + dot(ah, bl) + dot(al, bh)


def _sigmoid(x):
    return 1.0 / (1.0 + jnp.exp(-x))


def _silu(x):
    return x * _sigmoid(x)


def _mod_kernel(c_ref, w_ref, b_ref, o_ref):
    sc = _silu(c_ref[...])
    o_ref[0] = jnp.dot(sc, w_ref[0], precision=lax.Precision.HIGHEST,
                       preferred_element_type=F32) + b_ref[0]


def _adaln_mod(c, ada_w, ada_b):
    L, D, D3 = ada_w.shape
    B = c.shape[0]
    nj = D3 // D
    return pl.pallas_call(
        _mod_kernel,
        grid=(L, nj),
        in_specs=[
            pl.BlockSpec((B, D), lambda l, j: (0, 0)),
            pl.BlockSpec((1, D, D), lambda l, j: (l, 0, j)),
            pl.BlockSpec((1, 1, D), lambda l, j: (l, 0, j)),
        ],
        out_specs=pl.BlockSpec((1, B, D), lambda l, j: (l, 0, j)),
        out_shape=jax.ShapeDtypeStruct((L, B, D3), F32),
        compiler_params=_cparams(("arbitrary", "arbitrary")),
        name="adaln_mod",
    )(c, ada_w, ada_b.reshape(L, 1, D3))


def _in_kernel(x_ref, mod_ref, npre_ref, wq_ref, wk_ref, wv_ref, wf_ref, wz_ref, wp_ref, wg_ref,
               fb_ref, qg_ref, kg_ref, ltri_ref,
               q_out, k_out, v_out, z_out, p_out, g_out, carry_ref):
    si = pl.program_id(1)

    @pl.when(si == 0)
    def _():
        carry_ref[...] = jnp.zeros_like(carry_ref)

    D = D_MODEL
    x = x_ref[0]
    ts = x.shape[0]
    ms = jnp.mean(x * x, axis=-1, keepdims=True)
    xn = x * lax.rsqrt(ms + NORM_EPS) * npre_ref[...]
    mod = mod_ref[0]
    shift = mod[:, 0:D]
    scale = mod[:, D:2 * D]
    h = (xn * (1.0 + scale) + shift).astype(BF16)

    f = _dot(h, wf_ref[...]) + fb_ref[...]
    logf = (jnp.minimum(f, 0.0) - jnp.log1p(jnp.exp(-jnp.abs(f)))) * LOG2E
    cum = _dot_exact_lhs(ltri_ref[...], logf) + carry_ref[...]
    carry_ref[...] = cum[ts - 1:ts, :]
    c1 = cum.astype(BF16).astype(F32)
    r1 = cum - c1
    c2 = r1.astype(BF16).astype(F32)
    c3 = r1 - c2

    uq = _dot(h, wq_ref[...])
    uk = _dot(h, wk_ref[...])
    uv = _dot(h, wv_ref[...])

    lane = lax.broadcasted_iota(jnp.int32, (ts, LANES), 1)
    lo = lane < HEAD_DIM
    qscale = (HEAD_DIM ** -0.5) * LOG2E

    def head_norm(x2, gain2):
        sq = x2 * x2
        ss_lo = jnp.sum(jnp.where(lo, sq, 0.0), axis=-1, keepdims=True)
        ss_hi = jnp.sum(jnp.where(lo, 0.0, sq), axis=-1, keepdims=True)
        inv = jnp.where(lo, lax.rsqrt(ss_lo * (1.0 / HEAD_DIM) + NORM_EPS),
                        lax.rsqrt(ss_hi * (1.0 / HEAD_DIM) + NORM_EPS))
        return x2 * inv * gain2

    def extras(e0, hcol, for_q):
        a1 = c1[:, hcol:hcol + 1]
        a2 = c2[:, hcol:hcol + 1]
        a3 = c3[:, hcol:hcol + 1]
        one = jnp.ones((ts, 1), F32)
        if for_q:
            vals = (a1, a2, a3, one, one, one)
        else:
            vals = (one, one, one, -a1, -a2, -a3)
        e = jnp.zeros((ts, LANES), F32)
        for i, val in enumerate(vals):
            e = jnp.where(lane == e0 + i, val, e)
        return e

    for j in range(FOX_HEADS // 2):
        sl = slice(LANES * j, LANES * (j + 1))
        qn = head_norm(uq[:, sl], qg_ref[...]) * qscale
        kn = head_norm(uk[:, sl], kg_ref[...])
        v2 = uv[:, sl]
        he, ho = 2 * j, 2 * j + 1
        q_out[0, he] = jnp.where(lo, qn, extras(HEAD_DIM, he, True)).astype(BF16)
        k_out[0, he] = jnp.where(lo, kn, extras(HEAD_DIM, he, False)).astype(BF16)
        v_out[0, he] = jnp.where(lo, v2, jnp.where(lane == HEAD_DIM, 1.0, 0.0)).T.astype(BF16)
        q_out[0, ho] = jnp.where(lo, extras(0, ho, True), qn).astype(BF16)
        k_out[0, ho] = jnp.where(lo, extras(0, ho, False), kn).astype(BF16)
        v_out[0, ho] = jnp.where(lo, jnp.where(lane == 0, 1.0, 0.0), v2).T.astype(BF16)

    z_out[0] = _dot(h, wz_ref[...])
    p_out[0] = _dot(h, wp_ref[...])
    g_out[0] = _dot(h, wg_ref[...]).astype(BF16)


def _in_proj(x, mod_l, npre, wq, wk, wv, wf, wz, wp, wg, fbias, qg2, kg2, ts):
    B, S, D = x.shape
    ns = S // ts
    ltri = jnp.tril(jnp.ones((ts, ts), F32)).astype(BF16)
    full = lambda a: pl.BlockSpec(a.shape, lambda b, s: (0,) * a.ndim)
    hs = jax.ShapeDtypeStruct((B, FOX_HEADS, S, LANES), BF16)
    head_spec = pl.BlockSpec((1, FOX_HEADS, ts, LANES), lambda b, s: (b, 0, s, 0))
    seq_spec = lambda w: pl.BlockSpec((1, ts, w), lambda b, s: (b, s, 0))
    return pl.pallas_call(
        _in_kernel,
        grid=(B, ns),
        in_specs=[
            seq_spec(D),
            pl.BlockSpec((1, 1, 3 * D), lambda b, s: (b, 0, 0)),
            full(npre), full(wq), full(wk), full(wv), full(wf), full(wz), full(wp), full(wg),
            full(fbias), full(qg2), full(kg2), full(ltri),
        ],
        out_specs=[head_spec, head_spec,
                   pl.BlockSpec((1, FOX_HEADS, LANES, ts), lambda b, s: (b, 0, 0, s)),
                   seq_spec(RWKV_SHIFT_W), seq_spec(POOL_W), seq_spec(D_MIX)],
        out_shape=[hs, hs, jax.ShapeDtypeStruct((B, FOX_HEADS, LANES, S), BF16),
                   jax.ShapeDtypeStruct((B, S, RWKV_SHIFT_W), F32),
                   jax.ShapeDtypeStruct((B, S, POOL_W), F32),
                   jax.ShapeDtypeStruct((B, S, D_MIX), BF16)],
        scratch_shapes=[pltpu.VMEM((1, LANES), F32)],
        compiler_params=_cparams(("arbitrary", "arbitrary")),
        name="in_proj",
    )(x, mod_l, npre, wq, wk, wv, wf, wz, wp, wg, fbias, qg2, kg2, ltri)


def _attn_kernel(q_ref, k_ref, vt_ref, o_ref, *, tk):
    qi = pl.program_id(2)
    tq = 2 * tk
    heads = range(2)

    def scores_t(hh, q, blk, w):
        off = pl.multiple_of(blk * w, w)
        return _dot_nt(k_ref[0, hh, pl.ds(off, w), :], q)

    def update(hh, st, blk, w, m, acc):
        off = pl.multiple_of(blk * w, w)
        m_new = jnp.maximum(m, jnp.max(st, axis=0, keepdims=True))
        alpha = jnp.exp2(m - m_new)
        p = jnp.exp2(st - m_new).astype(BF16)
        acc = alpha * acc + _dot(vt_ref[0, hh, :, pl.ds(off, w)], p)
        return m_new, acc

    m0 = jnp.full((1, tq), NEG_BIG, F32)
    a0 = jnp.zeros((LANES, tq), F32)

    def body(j, carry):
        ms, accs_ = carry
        ss = [scores_t(hh, q_ref[0, hh], j, tq) for hh in heads]
        upd = [update(hh, ss[hh], j, tq, ms[hh], accs_[hh]) for hh in heads]
        return tuple(u[0] for u in upd), tuple(u[1] for u in upd)

    ms, accs = lax.fori_loop(0, qi, body, ((m0, m0), (a0, a0)))

    key_u = lax.broadcasted_iota(jnp.int32, (tk, tk), 0)
    qry_u = lax.broadcasted_iota(jnp.int32, (tk, tk), 1)
    key_l = lax.broadcasted_iota(jnp.int32, (tq, tk), 0)
    qry_l = lax.broadcasted_iota(jnp.int32, (tq, tk), 1)
    s_up = [jnp.where(key_u <= qry_u, scores_t(hh, q_ref[0, hh, 0:tk, :], 2 * qi, tk), NEG_BIG) for hh in heads]
    s_lo = [jnp.where(key_l <= qry_l + tk, scores_t(hh, q_ref[0, hh, tk:tq, :], qi, tq), NEG_BIG) for hh in heads]
    up = [update(hh, s_up[hh], 2 * qi, tk, ms[hh][:, 0:tk], accs[hh][:, 0:tk])[1] for hh in heads]
    low = [update(hh, s_lo[hh], qi, tq, ms[hh][:, tk:tq], accs[hh][:, tk:tq])[1] for hh in heads]
    accs = [jnp.concatenate([up[hh], low[hh]], axis=1) for hh in heads]

    l_e = accs[0][HEAD_DIM:HEAD_DIM + 1, :]
    l_o = accs[1][0:1, :]
    chan = lax.broadcasted_iota(jnp.int32, (LANES, tq), 0)
    o_t = jnp.where(chan < HEAD_DIM, accs[0] / l_e, accs[1] / l_o)
    o_ref[0] = o_t.T.astype(o_ref.dtype)


def _fox_attention(q, k, v, tk):
    B, H, S, _ = q.shape
    tq = 2 * tk
    nq = S // tq
    return pl.pallas_call(
        functools.partial(_attn_kernel, tk=tk),
        grid=(B, H // 2, nq),
        in_specs=[
            pl.BlockSpec((1, 2, tq, LANES), lambda b, p, i: (b, p, i, 0)),
            pl.BlockSpec((1, 2, S, LANES), lambda b, p, i: (b, p, 0, 0)),
            pl.BlockSpec((1, 2, LANES, S), lambda b, p, i: (b, p, 0, 0)),
        ],
        out_specs=pl.BlockSpec((1, tq, LANES), lambda b, p, i: (b, i, p)),
        out_shape=jax.ShapeDtypeStruct((B, S, FOX_W), BF16),
        compiler_params=_cparams(("arbitrary", "arbitrary", "arbitrary")),
        name="fox_attention",
    )(q, k, v)


def _mm(a, b):
    return _dot(a.astype(BF16), b.astype(BF16))


def _mm_nt(a, b):
    return _dot_nt(a.astype(BF16), b.astype(BF16))


def _mm_tn(a, b):
    return _dot_tn(a.astype(BF16), b.astype(BF16))


def _unit_lower_inverses(ms, row, col):
    eye = (row == col).astype(F32)
    same = lambda sh: (row >> sh) == (col >> sh)
    ts = [eye + jnp.where(same(1), m, 0.0) for m in ms]
    for sh in range(1, 6):
        band = jnp.logical_and(same(sh + 1), jnp.logical_not(same(sh)))
        ets = [_mm(jnp.where(band, m, 0.0), t) for m, t in zip(ms, ts)]
        ts = [t + _mm(t, et) for t, et in zip(ts, ets)]
    return ts


def _rwkv_kernel(z_ref, mu_ref, w0_ref, a0_ref, kk_ref, ka_ref, rk_ref, lng_ref, lnb_ref,
                 w2a2_ref, seg_ref, ltri_ref, y_ref, state_ref, prev_ref):
    si = pl.program_id(1)

    @pl.when(si == 0)
    def _():
        state_ref[...] = jnp.zeros_like(state_ref)
        prev_ref[...] = jnp.zeros_like(prev_ref)

    W = RWKV_W
    N = HEAD_DIM
    z = z_ref[0]
    cb = z.shape[0]
    row = lax.broadcasted_iota(jnp.int32, (cb, 1), 0)
    zprev = jnp.where(row == 0, prev_ref[...], pltpu.roll(z, 1, 0))
    prev_ref[...] = z[cb - 1:cb, :]
    zf = z + (zprev - z) * mu_ref[...]
    r = zf[:, 0:W]
    k = zf[:, W:2 * W]
    v = zf[:, 2 * W:3 * W]
    t6 = zf[:, 3 * W:3 * W + 2 * N]
    lane = lax.broadcasted_iota(jnp.int32, (cb, 2 * N), 1)
    tz = jnp.where(lane < N, jnp.tanh(t6), t6)
    dwa = _dot3(tz, w2a2_ref[...])
    logw = -DECAY_SCALE * _sigmoid(w0_ref[...] + dwa[:, 0:W])
    a = _sigmoid(a0_ref[...] + dwa[:, W:2 * W])

    seg = seg_ref[...]
    kkr = k * kk_ref[...]
    kk = kkr / jnp.maximum(jnp.sqrt(_dot_wide_rhs(kkr * kkr, seg)), 1e-12)
    k2 = k * (1.0 + (a - 1.0) * ka_ref[...])
    bonus = _dot_wide_rhs(r * k2 * rk_ref[...], seg) * v

    logg = _dot_exact_lhs(ltri_ref[...], logw)
    g = jnp.exp(logg)
    gex = jnp.exp(logg - logw)
    gi = jnp.exp(-logg)
    at = -kk * gex
    rt = r * g
    bt = kk * a * gi
    kt = k2 * gi

    rw = lax.broadcasted_iota(jnp.int32, (W, W), 0)
    cw = lax.broadcasted_iota(jnp.int32, (W, W), 1)
    head_blk = (rw >> 6) == (cw >> 6)
    tw = lax.broadcasted_iota(jnp.int32, (CHUNK, W), 0)
    sw = lax.broadcasted_iota(jnp.int32, (CHUNK, W), 1) & (CHUNK - 1)
    strict_w = sw < tw
    incl_w = sw <= tw

    def tall_f32(zc):
        return jnp.where(head_blk, jnp.concatenate([zc] * RWKV_HEADS, axis=0), 0.0)

    def tall(zc):
        return tall_f32(zc).astype(BF16)

    def fold(y):
        return y[0:CHUNK] + y[CHUNK:2 * CHUNK] + y[2 * CHUNK:3 * CHUNK] + y[3 * CHUNK:4 * CHUNK]

    nchunk = cb // CHUNK
    chunks = range(nchunk)
    rows = [slice(CHUNK * c, CHUNK * (c + 1)) for c in chunks]
    glast = [g[CHUNK * (c + 1) - 1:CHUNK * (c + 1), :] for c in chunks]
    t_a = [tall(at[rs]) for rs in rows]
    t_bk = [jnp.concatenate([tall(bt[rs]), tall(kt[rs])], axis=0) for rs in rows]
    t_v = [tall(v[rs]) for rs in rows]
    pp = [_dot_nt(jnp.concatenate([at[rows[c]], rt[rows[c]]], axis=0).astype(BF16), t_bk[c]) for c in chunks]
    m_ab = [tall_f32(jnp.where(strict_w, p[0:CHUNK, 0:W], 0.0)) for p in pp]
    m_akrk = [jnp.concatenate([jnp.where(strict_w, p[0:CHUNK, W:2 * W], 0.0),
                               jnp.where(incl_w, p[CHUNK:2 * CHUNK, W:2 * W], 0.0)], axis=0) for p in pp]
    m_rb = [jnp.where(incl_w, p[CHUNK:2 * CHUNK, 0:W], 0.0).astype(BF16) for p in pp]
    xo = [_mm(m_akrk[c], t_v[c]) for c in chunks]
    d0 = [jnp.where(head_blk, _mm_tn(v[rows[c]], kt[rows[c]] * glast[c]), 0.0) for c in chunks]
    bh = [(bt[rows[c]] * glast[c]).astype(BF16) for c in chunks]
    tinv = _unit_lower_inverses(m_ab, rw, cw)
    wu = [_mm(fold(tinv[c]), jnp.concatenate([t_a[c], tall(xo[c][0:CHUNK])], axis=1)) for c in chunks]
    w_all = [x[:, 0:W] for x in wu]
    u0_all = [x[:, W:2 * W] for x in wu]
    o0_all = [x[CHUNK:2 * CHUNK] for x in xo]
    pq = [_dot_tn(wu[c].astype(BF16), bh[c]) for c in chunks]
    phi = [jnp.where(head_blk, x[0:W], 0.0).astype(BF16) for x in pq]
    psi = [jnp.where(head_blk, pq[c][W:2 * W], 0.0) + d0[c] for c in chunks]
    mw = [_dot(m_rb[c], jnp.concatenate([tall(w_all[c]), tall(u0_all[c])], axis=1)) for c in chunks]
    omega = [(rt[rows[c]] + mw[c][:, 0:W]).astype(BF16) for c in chunks]
    xi = [mw[c][:, W:2 * W] + o0_all[c] for c in chunks]

    outs = []
    s_bd = state_ref[...]
    for c in chunks:
        s_bf = s_bd.astype(BF16)
        outs.append(_dot_nt(omega[c], s_bf) + xi[c])
        s_bd = s_bd * glast[c] + _dot(s_bf, phi[c]) + psi[c]
    state_ref[...] = s_bd

    o = jnp.concatenate(outs, axis=0)
    mean = _dot_wide_rhs(o, seg) * (1.0 / N)
    d = o - mean
    var = _dot_wide_rhs(d * d, seg) * (1.0 / N)
    y_ref[0] = d * lax.rsqrt(var + RWKV_GN_EPS) * lng_ref[...] + lnb_ref[...] + bonus


def _rwkv(z, mu, w0, a0, k_k, k_a, r_k, ln_g, ln_b, w2a2, cb):
    B, S, _ = z.shape
    ns = S // cb
    seg = jnp.kron(jnp.eye(RWKV_HEADS, dtype=F32), jnp.ones((HEAD_DIM, HEAD_DIM), F32)).astype(BF16)
    ltri = jnp.kron(jnp.eye(cb // CHUNK, dtype=F32), jnp.tril(jnp.ones((CHUNK, CHUNK), F32))).astype(BF16)
    full = lambda a: pl.BlockSpec(a.shape, lambda b, s: (0,) * a.ndim)
    args = (mu, w0, a0, k_k, k_a, r_k, ln_g, ln_b, w2a2, seg, ltri)
    return pl.pallas_call(
        _rwkv_kernel,
        grid=(B, ns),
        in_specs=[pl.BlockSpec((1, cb, RWKV_SHIFT_W), lambda b, s: (b, s, 0))] + [full(a) for a in args],
        out_specs=pl.BlockSpec((1, cb, RWKV_W), lambda b, s: (b, s, 0)),
        out_shape=jax.ShapeDtypeStruct((B, S, RWKV_W), F32),
        scratch_shapes=[pltpu.VMEM((RWKV_W, RWKV_W), F32),
                        pltpu.VMEM((1, RWKV_SHIFT_W), F32)],
        compiler_params=_cparams(("arbitrary", "arbitrary")),
        name="rwkv7",
    )(z, *args)


def _out_kernel(x_ref, mod_ref, yf_ref, yr_ref, p_ref, g_ref, pw_ref, ps_ref, wout_ref, npost_ref,
                o_ref, halo_ref):
    si = pl.program_id(1)

    @pl.when(si == 0)
    def _():
        halo_ref[...] = jnp.zeros_like(halo_ref)

    D = D_MODEL
    p = p_ref[0]
    ts = p.shape[0]
    e = jnp.concatenate([halo_ref[...], p], axis=0)
    halo_ref[...] = p[ts - POOL_HALO:ts, :]
    sums = []
    s_run = e
    shift = 1
    for _w in POOL_WINDOWS:
        s_run = s_run + pltpu.roll(s_run, shift, 0)
        sums.append(s_run[POOL_HALO:, :])
        shift *= 2
    pos = si * ts + lax.broadcasted_iota(jnp.int32, (ts, 1), 0) + 1
    lane = lax.broadcasted_iota(jnp.int32, (ts, POOL_W), 1)
    pooled = jnp.zeros((ts, POOL_W), F32)
    for gi, w in enumerate(POOL_WINDOWS):
        cnt = jnp.minimum(pos, w).astype(F32)
        grp = (lane >= POOL_GROUP_DIM * gi) & (lane < POOL_GROUP_DIM * (gi + 1))
        pooled = jnp.where(grp, sums[gi] / cnt, pooled)
    pooled = pooled - p
    y_pool = _dot(pooled.astype(BF16), pw_ref[...]) * ps_ref[...]

    y = jnp.concatenate([yf_ref[0].astype(F32), yr_ref[0], y_pool], axis=-1) * _silu(g_ref[0].astype(F32))
    t = _dot(y.astype(BF16), wout_ref[...])
    ms = jnp.mean(t * t, axis=-1, keepdims=True)
    tn = t * lax.rsqrt(ms + NORM_EPS) * npost_ref[...]
    gate = mod_ref[0][:, 2 * D:3 * D]
    o_ref[0] = x_ref[0] + gate * tn


def _out_proj(x, mod_l, y_fox, y_rwkv, p, g, pw_bd, pool_scale, w_out, npost, ts):
    B, S, D = x.shape
    ns = S // ts
    full = lambda a: pl.BlockSpec(a.shape, lambda b, s: (0,) * a.ndim)
    seq_spec = lambda w: pl.BlockSpec((1, ts, w), lambda b, s: (b, s, 0))
    return pl.pallas_call(
        _out_kernel,
        grid=(B, ns),
        in_specs=[seq_spec(D), pl.BlockSpec((1, 1, 3 * D), lambda b, s: (b, 0, 0)),
                  seq_spec(FOX_W), seq_spec(RWKV_W), seq_spec(POOL_W), seq_spec(D_MIX),
                  full(pw_bd), full(pool_scale), full(w_out), full(npost)],
        out_specs=seq_spec(D),
        out_shape=jax.ShapeDtypeStruct((B, S, D), F32),
        scratch_shapes=[pltpu.VMEM((POOL_HALO, POOL_W), F32)],
        compiler_params=_cparams(("arbitrary", "arbitrary")),
        name="out_proj",
    )(x, mod_l, y_fox, y_rwkv, p, g, pw_bd, pool_scale, w_out, npost)


def _block_diag(blocks):
    n = blocks.shape[0]
    d = blocks.shape[1]
    out = jnp.zeros((n * d, n * d), blocks.dtype)
    for i in range(n):
        out = out.at[i * d:(i + 1) * d, i * d:(i + 1) * d].set(blocks[i])
    return out


def kernel(x, c, ada_w, ada_b, norm_pre, norm_post, w_in, fox_q_gain, fox_k_gain, fox_f_bias, rwkv_mu, rwkv_w0, rwkv_w2, rwkv_a0, rwkv_a2, rwkv_k_k, rwkv_k_a, rwkv_r_k, rwkv_ln_g, rwkv_ln_b, pool_w, pool_scale, w_out):
    B, S, D = x.shape
    L = ada_w.shape[0]
    ts = min(512, S)
    tk = min(512, S // 2)
    cb = min(256, S)

    mod = _adaln_mod(c, ada_w, ada_b)
    o_q, o_k, o_v, o_f, o_z, o_p = (0, FOX_W, 2 * FOX_W, 3 * FOX_W, 3 * FOX_W + FOX_HEADS,
                                    3 * FOX_W + FOX_HEADS + RWKV_SHIFT_W)
    o_g = o_p + POOL_W
    row = lambda a: a.reshape(1, -1)

    for l in range(L):
        w = w_in[l]
        wq = w[:, o_q:o_k].astype(BF16)
        wk = w[:, o_k:o_v].astype(BF16)
        wv = w[:, o_v:o_f].astype(BF16)
        wf = jnp.pad(w[:, o_f:o_z], ((0, 0), (0, F_PAD - FOX_HEADS))).astype(BF16)
        wz = w[:, o_z:o_p].astype(BF16)
        wp = w[:, o_p:o_g].astype(BF16)
        wg = w[:, o_g:].astype(BF16)
        fbias = jnp.pad(fox_f_bias[l], (0, F_PAD - FOX_HEADS)).reshape(1, F_PAD)
        qg2 = jnp.tile(fox_q_gain[l], 2).reshape(1, LANES)
        kg2 = jnp.tile(fox_k_gain[l], 2).reshape(1, LANES)
        mod_l = mod[l].reshape(B, 1, 3 * D)

        q_aug, k_aug, v_aug, z, p, g = _in_proj(x, mod_l, row(norm_pre[l]), wq, wk, wv, wf, wz, wp, wg,
                                                fbias, qg2, kg2, ts)
        y_fox = _fox_attention(q_aug, k_aug, v_aug, tk)

        w2a2 = jnp.zeros((2 * HEAD_DIM, 2 * RWKV_W), F32)
        w2a2 = w2a2.at[:RWKV_DECAY_RANK, :RWKV_W].set(rwkv_w2[l])
        w2a2 = w2a2.at[RWKV_DECAY_RANK:, RWKV_W:].set(rwkv_a2[l])
        y_rwkv = _rwkv(z, row(rwkv_mu[l]), row(rwkv_w0[l]), row(rwkv_a0[l]), row(rwkv_k_k[l]),
                       row(rwkv_k_a[l]), row(rwkv_r_k[l]), row(rwkv_ln_g[l]), row(rwkv_ln_b[l]),
                       w2a2, cb)

        pw_bd = _block_diag(pool_w[l]).astype(BF16)
        x = _out_proj(x, mod_l, y_fox, y_rwkv, p, g, pw_bd, row(pool_scale[l]),
                      w_out[l].astype(BF16), row(norm_post[l]), ts)
    return x
```

```python
import functools
import math

import jax
import jax.numpy as jnp
from jax import lax
from jax.experimental import pallas as pl
from jax.experimental.pallas import tpu as pltpu

F32 = jnp.float32
BF16 = jnp.bfloat16

D_MODEL = 1024
HEAD_DIM = 64
FOX_HEADS = 8
FOX_W = FOX_HEADS * HEAD_DIM
RWKV_HEADS = 4
RWKV_W = RWKV_HEADS * HEAD_DIM
POOL_GROUPS = 4
POOL_GROUP_DIM = 64
POOL_W = POOL_GROUPS * POOL_GROUP_DIM
POOL_WINDOWS = (2, 4, 8, 16)
POOL_HALO = 16
D_MIX = FOX_W + RWKV_W + POOL_W
RWKV_DECAY_RANK = 64
RWKV_ICL_RANK = 64
RWKV_SHIFT_W = 3 * RWKV_W + RWKV_DECAY_RANK + RWKV_ICL_RANK
NORM_EPS = 1e-6
RWKV_GN_EPS = 64e-5
DECAY_SCALE = float(math.exp(-0.5))
LOG2E = float(1.0 / math.log(2.0))

LANES = 128
F_PAD = LANES
CHUNK = 64
NEG_BIG = -1e30

VMEM_LIMIT = 56 * 1024 * 1024


def _cparams(sem):
    return pltpu.CompilerParams(dimension_semantics=sem, vmem_limit_bytes=VMEM_LIMIT)


def _dot(a, b):
    return jnp.dot(a, b, preferred_element_type=F32)


def _dot_nt(a, b):
    return lax.dot_general(a, b, (((1,), (1,)), ((), ())), preferred_element_type=F32)


def _dot_tn(a, b):
    return lax.dot_general(a, b, (((0,), (0,)), ((), ())), preferred_element_type=F32)


def _split2(x):
    hi = x.astype(BF16)
    lo = (x - hi.astype(F32)).astype(BF16)
    return hi, lo


def _split3(x):
    hi = x.astype(BF16)
    r = x - hi.astype(F32)
    mid = r.astype(BF16)
    lo = (r - mid.astype(F32)).astype(BF16)
    return hi, mid, lo


def _dot_exact_lhs(l_bf16, x):
    h, m, l = _split3(x)
    return _dot(l_bf16, h) + _dot(l_bf16, m) + _dot(l_bf16, l)


def _dot_wide_lhs(l_bf16, x):
    h, l = _split2(x)
    return _dot(l_bf16, h) + _dot(l_bf16, l)


def _dot_wide_rhs(x, r_bf16):
    h, l = _split2(x)
    return _dot(h, r_bf16) + _dot(l, r_bf16)


def _dot3(a, b, dot=_dot):
    ah, al = _split2(a)
    bh, bl = _split2(b)
    return dot(ah, bh) + dot(ah, bl) + dot(al, bh)


def _sigmoid(x):
    return 1.0 / (1.0 + jnp.exp(-x))


def _silu(x):
    return x * _sigmoid(x)


def _mod_kernel(c_ref, w_ref, b_ref, o_ref):
    sc = _silu(c_ref[...])
    o_ref[0] = jnp.dot(sc, w_ref[0], precision=lax.Precision.HIGHEST,
                       preferred_element_type=F32) + b_ref[0]


def _adaln_mod(c, ada_w, ada_b):
    L, D, D3 = ada_w.shape
    B = c.shape[0]
    nj = D3 // D
    return pl.pallas_call(
        _mod_kernel,
        grid=(L, nj),
        in_specs=[
            pl.BlockSpec((B, D), lambda l, j: (0, 0)),
            pl.BlockSpec((1, D, D), lambda l, j: (l, 0, j)),
            pl.BlockSpec((1, 1, D), lambda l, j: (l, 0, j)),
        ],
        out_specs=pl.BlockSpec((1, B, D), lambda l, j: (l, 0, j)),
        out_shape=jax.ShapeDtypeStruct((L, B, D3), F32),
        compiler_params=_cparams(("arbitrary", "arbitrary")),
        name="adaln_mod",
    )(c, ada_w, ada_b.reshape(L, 1, D3))


def _in_kernel(x_ref, mod_ref, npre_ref, wq_ref, wk_ref, wv_ref, wf_ref, wz_ref, wp_ref, wg_ref,
               fb_ref, qg_ref, kg_ref, ltri_ref,
               q_out, k_out, v_out, z_out, p_out, g_out, carry_ref):
    si = pl.program_id(1)

    @pl.when(si == 0)
    def _():
        carry_ref[...] = jnp.zeros_like(carry_ref)

    D = D_MODEL
    x = x_ref[0]
    ts = x.shape[0]
    ms = jnp.mean(x * x, axis=-1, keepdims=True)
    xn = x * lax.rsqrt(ms + NORM_EPS) * npre_ref[...]
    mod = mod_ref[0]
    shift = mod[:, 0:D]
    scale = mod[:, D:2 * D]
    h = (xn * (1.0 + scale) + shift).astype(BF16)

    f = _dot(h, wf_ref[...]) + fb_ref[...]
    logf = (jnp.minimum(f, 0.0) - jnp.log1p(jnp.exp(-jnp.abs(f)))) * LOG2E
    cum = _dot_exact_lhs(ltri_ref[...], logf) + carry_ref[...]
    carry_ref[...] = cum[ts - 1:ts, :]
    c1 = cum.astype(BF16).astype(F32)
    r1 = cum - c1
    c2 = r1.astype(BF16).astype(F32)
    c3 = r1 - c2

    uq = _dot(h, wq_ref[...])
    uk = _dot(h, wk_ref[...])
    uv = _dot(h, wv_ref[...])

    lane = lax.broadcasted_iota(jnp.int32, (ts, LANES), 1)
    lo = lane < HEAD_DIM
    qscale = (HEAD_DIM ** -0.5) * LOG2E

    def head_norm(x2, gain2):
        sq = x2 * x2
        ss_lo = jnp.sum(jnp.where(lo, sq, 0.0), axis=-1, keepdims=True)
        ss_hi = jnp.sum(jnp.where(lo, 0.0, sq), axis=-1, keepdims=True)
        inv = jnp.where(lo, lax.rsqrt(ss_lo * (1.0 / HEAD_DIM) + NORM_EPS),
                        lax.rsqrt(ss_hi * (1.0 / HEAD_DIM) + NORM_EPS))
        return x2 * inv * gain2

    def extras(e0, hcol, for_q):
        a1 = c1[:, hcol:hcol + 1]
        a2 = c2[:, hcol:hcol + 1]
        a3 = c3[:, hcol:hcol + 1]
        one = jnp.ones((ts, 1), F32)
        if for_q:
            vals = (a1, a2, a3, one, one, one)
        else:
            vals = (one, one, one, -a1, -a2, -a3)
        e = jnp.zeros((ts, LANES), F32)
        for i, val in enumerate(vals):
            e = jnp.where(lane == e0 + i, val, e)
        return e

    for j in range(FOX_HEADS // 2):
        sl = slice(LANES * j, LANES * (j + 1))
        qn = head_norm(uq[:, sl], qg_ref[...]) * qscale
        kn = head_norm(uk[:, sl], kg_ref[...])
        v2 = uv[:, sl]
        he, ho = 2 * j, 2 * j + 1
        q_out[0, he] = jnp.where(lo, qn, extras(HEAD_DIM, he, True)).astype(BF16)
        k_out[0, he] = jnp.where(lo, kn, extras(HEAD_DIM, he, False)).astype(BF16)
        v_out[0, he] = jnp.where(lo, v2, jnp.where(lane == HEAD_DIM, 1.0, 0.0)).T.astype(BF16)
        q_out[0, ho] = jnp.where(lo, extras(0, ho, True), qn).astype(BF16)
        k_out[0, ho] = jnp.where(lo, extras(0, ho, False), kn).astype(BF16)
        v_out[0, ho] = jnp.where(lo, jnp.where(lane == 0, 1.0, 0.0), v2).T.astype(BF16)

    z_out[0] = _dot(h, wz_ref[...])
    p_out[0] = _dot(h, wp_ref[...])
    g_out[0] = _dot(h, wg_ref[...]).astype(BF16)


def _in_proj(x, mod_l, npre, wq, wk, wv, wf, wz, wp, wg, fbias, qg2, kg2, ts):
    B, S, D = x.shape
    ns = S // ts
    ltri = jnp.tril(jnp.ones((ts, ts), F32)).astype(BF16)
    full = lambda a: pl.BlockSpec(a.shape, lambda b, s: (0,) * a.ndim)
    hs = jax.ShapeDtypeStruct((B, FOX_HEADS, S, LANES), BF16)
    head_spec = pl.BlockSpec((1, FOX_HEADS, ts, LANES), lambda b, s: (b, 0, s, 0))
    seq_spec = lambda w: pl.BlockSpec((1, ts, w), lambda b, s: (b, s, 0))
    return pl.pallas_call(
        _in_kernel,
        grid=(B, ns),
        in_specs=[
            seq_spec(D),
            pl.BlockSpec((1, 1, 3 * D), lambda b, s: (b, 0, 0)),
            full(npre), full(wq), full(wk), full(wv), full(wf), full(wz), full(wp), full(wg),
            full(fbias), full(qg2), full(kg2), full(ltri),
        ],
        out_specs=[head_spec, head_spec,
                   pl.BlockSpec((1, FOX_HEADS, LANES, ts), lambda b, s: (b, 0, 0, s)),
                   seq_spec(RWKV_SHIFT_W), seq_spec(POOL_W), seq_spec(D_MIX)],
        out_shape=[hs, hs, jax.ShapeDtypeStruct((B, FOX_HEADS, LANES, S), BF16),
                   jax.ShapeDtypeStruct((B, S, RWKV_SHIFT_W), F32),
                   jax.ShapeDtypeStruct((B, S, POOL_W), F32),
                   jax.ShapeDtypeStruct((B, S, D_MIX), BF16)],
        scratch_shapes=[pltpu.VMEM((1, LANES), F32)],
        compiler_params=_cparams(("arbitrary", "arbitrary")),
        name="in_proj",
    )(x, mod_l, npre, wq, wk, wv, wf, wz, wp, wg, fbias, qg2, kg2, ltri)


def _attn_kernel(q_ref, k_ref, vt_ref, o_ref, *, tk):
    qi = pl.program_id(2)
    tq = 2 * tk
    heads = range(2)

    def scores_t(hh, q, blk, w):
        off = pl.multiple_of(blk * w, w)
        return _dot_nt(k_ref[0, hh, pl.ds(off, w), :], q)

    def update(hh, st, blk, w, m, acc):
        off = pl.multiple_of(blk * w, w)
        m_new = jnp.maximum(m, jnp.max(st, axis=0, keepdims=True))
        alpha = jnp.exp2(m - m_new)
        p = jnp.exp2(st - m_new).astype(BF16)
        acc = alpha * acc + _dot(vt_ref[0, hh, :, pl.ds(off, w)], p)
        return m_new, acc

    m0 = jnp.full((1, tq), NEG_BIG, F32)
    a0 = jnp.zeros((LANES, tq), F32)

    def body(j, carry):
        ms, accs_ = carry
        ss = [scores_t(hh, q_ref[0, hh], j, tq) for hh in heads]
        upd = [update(hh, ss[hh], j, tq, ms[hh], accs_[hh]) for hh in heads]
        return tuple(u[0] for u in upd), tuple(u[1] for u in upd)

    ms, accs = lax.fori_loop(0, qi, body, ((m0, m0), (a0, a0)))

    key_u = lax.broadcasted_iota(jnp.int32, (tk, tk), 0)
    qry_u = lax.broadcasted_iota(jnp.int32, (tk, tk), 1)
    key_l = lax.broadcasted_iota(jnp.int32, (tq, tk), 0)
    qry_l = lax.broadcasted_iota(jnp.int32, (tq, tk), 1)
    s_up = [jnp.where(key_u <= qry_u, scores_t(hh, q_ref[0, hh, 0:tk, :], 2 * qi, tk), NEG_BIG) for hh in heads]
    s_lo = [jnp.where(key_l <= qry_l + tk, scores_t(hh, q_ref[0, hh, tk:tq, :], qi, tq), NEG_BIG) for hh in heads]
    up = [update(hh, s_up[hh], 2 * qi, tk, ms[hh][:, 0:tk], accs[hh][:, 0:tk])[1] for hh in heads]
    low = [update(hh, s_lo[hh], qi, tq, ms[hh][:, tk:tq], accs[hh][:, tk:tq])[1] for hh in heads]
    accs = [jnp.concatenate([up[hh], low[hh]], axis=1) for hh in heads]

    l_e = accs[0][HEAD_DIM:HEAD_DIM + 1, :]
    l_o = accs[1][0:1, :]
    chan = lax.broadcasted_iota(jnp.int32, (LANES, tq), 0)
    o_t = jnp.where(chan < HEAD_DIM, accs[0] / l_e, accs[1] / l_o)
    o_ref[0] = o_t.T.astype(o_ref.dtype)


def _fox_attention(q, k, v, tk):
    B, H, S, _ = q.shape
    tq = 2 * tk
    nq = S // tq
    return pl.pallas_call(
        functools.partial(_attn_kernel, tk=tk),
        grid=(B, H // 2, nq),
        in_specs=[
            pl.BlockSpec((1, 2, tq, LANES), lambda b, p, i: (b, p, i, 0)),
            pl.BlockSpec((1, 2, S, LANES), lambda b, p, i: (b, p, 0, 0)),
            pl.BlockSpec((1, 2, LANES, S), lambda b, p, i: (b, p, 0, 0)),
        ],
        out_specs=pl.BlockSpec((1, tq, LANES), lambda b, p, i: (b, i, p)),
        out_shape=jax.ShapeDtypeStruct((B, S, FOX_W), BF16),
        compiler_params=_cparams(("arbitrary", "arbitrary", "arbitrary")),
        name="fox_attention",
    )(q, k, v)


def _mm(a, b):
    return _dot(a.astype(BF16), b.astype(BF16))


def _mm_nt(a, b):
    return _dot_nt(a.astype(BF16), b.astype(BF16))


def _mm_tn(a, b):
    return _dot_tn(a.astype(BF16), b.astype(BF16))


def _unit_lower_inverses(ms, row, col):
    eye = (row == col).astype(F32)
    same = lambda sh: (row >> sh) == (col >> sh)
    ts = [eye + jnp.where(same(1), m, 0.0) for m in ms]
    for sh in range(1, 6):
        band = jnp.logical_and(same(sh + 1), jnp.logical_not(same(sh)))
        ets = [_mm(jnp.where(band, m, 0.0), t) for m, t in zip(ms, ts)]
        ts = [t + _mm(t, et) for t, et in zip(ts, ets)]
    return ts


def _rwkv_kernel(z_ref, mu_ref, w0_ref, a0_ref, kk_ref, ka_ref, rk_ref, lng_ref, lnb_ref,
                 w2a2_ref, seg_ref, ltri_ref, y_ref, state_ref, prev_ref):
    si = pl.program_id(1)

    @pl.when(si == 0)
    def _():
        state_ref[...] = jnp.zeros_like(state_ref)
        prev_ref[...] = jnp.zeros_like(prev_ref)

    W = RWKV_W
    N = HEAD_DIM
    z = z_ref[0]
    cb = z.shape[0]
    row = lax.broadcasted_iota(jnp.int32, (cb, 1), 0)
    zprev = jnp.where(row == 0, prev_ref[...], pltpu.roll(z, 1, 0))
    prev_ref[...] = z[cb - 1:cb, :]
    zf = z + (zprev - z) * mu_ref[...]
    r = zf[:, 0:W]
    k = zf[:, W:2 * W]
    v = zf[:, 2 * W:3 * W]
    t6 = zf[:, 3 * W:3 * W + 2 * N]
    lane = lax.broadcasted_iota(jnp.int32, (cb, 2 * N), 1)
    tz = jnp.where(lane < N, jnp.tanh(t6), t6)
    dwa = _dot3(tz, w2a2_ref[...])
    logw = -DECAY_SCALE * _sigmoid(w0_ref[...] + dwa[:, 0:W])
    a = _sigmoid(a0_ref[...] + dwa[:, W:2 * W])

    seg = seg_ref[...]
    kkr = k * kk_ref[...]
    kk = kkr / jnp.maximum(jnp.sqrt(_dot_wide_rhs(kkr * kkr, seg)), 1e-12)
    k2 = k * (1.0 + (a - 1.0) * ka_ref[...])
    bonus = _dot_wide_rhs(r * k2 * rk_ref[...], seg) * v

    logg = _dot_exact_lhs(ltri_ref[...], logw)
    g = jnp.exp(logg)
    gex = jnp.exp(logg - logw)
    gi = jnp.exp(-logg)
    at = -kk * gex
    rt = r * g
    bt = kk * a * gi
    kt = k2 * gi

    rw = lax.broadcasted_iota(jnp.int32, (W, W), 0)
    cw = lax.broadcasted_iota(jnp.int32, (W, W), 1)
    head_blk = (rw >> 6) == (cw >> 6)
    tw = lax.broadcasted_iota(jnp.int32, (CHUNK, W), 0)
    sw = lax.broadcasted_iota(jnp.int32, (CHUNK, W), 1) & (CHUNK - 1)
    strict_w = sw < tw
    incl_w = sw <= tw

    def tall_f32(zc):
        return jnp.where(head_blk, jnp.concatenate([zc] * RWKV_HEADS, axis=0), 0.0)

    def tall(zc):
        return tall_f32(zc).astype(BF16)

    def fold(y):
        return y[0:CHUNK] + y[CHUNK:2 * CHUNK] + y[2 * CHUNK:3 * CHUNK] + y[3 * CHUNK:4 * CHUNK]

    nchunk = cb // CHUNK
    chunks = range(nchunk)
    rows = [slice(CHUNK * c, CHUNK * (c + 1)) for c in chunks]
    glast = [g[CHUNK * (c + 1) - 1:CHUNK * (c + 1), :] for c in chunks]
    t_a = [tall(at[rs]) for rs in rows]
    t_bk = [jnp.concatenate([tall(bt[rs]), tall(kt[rs])], axis=0) for rs in rows]
    t_v = [tall(v[rs]) for rs in rows]
    pp = [_dot_nt(jnp.concatenate([at[rows[c]], rt[rows[c]]], axis=0).astype(BF16), t_bk[c]) for c in chunks]
    m_ab = [tall_f32(jnp.where(strict_w, p[0:CHUNK, 0:W], 0.0)) for p in pp]
    m_akrk = [jnp.concatenate([jnp.where(strict_w, p[0:CHUNK, W:2 * W], 0.0),
                               jnp.where(incl_w, p[CHUNK:2 * CHUNK, W:2 * W], 0.0)], axis=0) for p in pp]
    m_rb = [jnp.where(incl_w, p[CHUNK:2 * CHUNK, 0:W], 0.0).astype(BF16) for p in pp]
    xo = [_mm(m_akrk[c], t_v[c]) for c in chunks]
    d0 = [jnp.where(head_blk, _mm_tn(v[rows[c]], kt[rows[c]] * glast[c]), 0.0) for c in chunks]
    bh = [(bt[rows[c]] * glast[c]).astype(BF16) for c in chunks]
    tinv = _unit_lower_inverses(m_ab, rw, cw)
    wu = [_mm(fold(tinv[c]), jnp.concatenate([t_a[c], tall(xo[c][0:CHUNK])], axis=1)) for c in chunks]
    w_all = [x[:, 0:W] for x in wu]
    u0_all = [x[:, W:2 * W] for x in wu]
    o0_all = [x[CHUNK:2 * CHUNK] for x in xo]
    pq = [_dot_tn(wu[c].astype(BF16), bh[c]) for c in chunks]
    phi = [jnp.where(head_blk, x[0:W], 0.0).astype(BF16) for x in pq]
    psi = [jnp.where(head_blk, pq[c][W:2 * W], 0.0) + d0[c] for c in chunks]
    mw = [_dot(m_rb[c], jnp.concatenate([tall(w_all[c]), tall(u0_all[c])], axis=1)) for c in chunks]
    omega = [(rt[rows[c]] + mw[c][:, 0:W]).astype(BF16) for c in chunks]
    xi = [mw[c][:, W:2 * W] + o0_all[c] for c in chunks]

    outs = []
    s_bd = state_ref[...]
    for c in chunks:
        s_bf = s_bd.astype(BF16)
        outs.append(_dot_nt(omega[c], s_bf) + xi[c])
        s_bd = s_bd * glast[c] + _dot(s_bf, phi[c]) + psi[c]
    state_ref[...] = s_bd

    o = jnp.concatenate(outs, axis=0)
    mean = _dot_wide_rhs(o, seg) * (1.0 / N)
    d = o - mean
    var = _dot_wide_rhs(d * d, seg) * (1.0 / N)
    y_ref[0] = d * lax.rsqrt(var + RWKV_GN_EPS) * lng_ref[...] + lnb_ref[...] + bonus


def _rwkv(z, mu, w0, a0, k_k, k_a, r_k, ln_g, ln_b, w2a2, cb):
    B, S, _ = z.shape
    ns = S // cb
    seg = jnp.kron(jnp.eye(RWKV_HEADS, dtype=F32), jnp.ones((HEAD_DIM, HEAD_DIM), F32)).astype(BF16)
    ltri = jnp.kron(jnp.eye(cb // CHUNK, dtype=F32), jnp.tril(jnp.ones((CHUNK, CHUNK), F32))).astype(BF16)
    full = lambda a: pl.BlockSpec(a.shape, lambda b, s: (0,) * a.ndim)
    args = (mu, w0, a0, k_k, k_a, r_k, ln_g, ln_b, w2a2, seg, ltri)
    return pl.pallas_call(
        _rwkv_kernel,
        grid=(B, ns),
        in_specs=[pl.BlockSpec((1, cb, RWKV_SHIFT_W), lambda b, s: (b, s, 0))] + [full(a) for a in args],
        out_specs=pl.BlockSpec((1, cb, RWKV_W), lambda b, s: (b, s, 0)),
        out_shape=jax.ShapeDtypeStruct((B, S, RWKV_W), F32),
        scratch_shapes=[pltpu.VMEM((RWKV_W, RWKV_W), F32),
                        pltpu.VMEM((1, RWKV_SHIFT_W), F32)],
        compiler_params=_cparams(("arbitrary", "arbitrary")),
        name="rwkv7",
    )(z, *args)


def _out_kernel(x_ref, mod_ref, yf_ref, yr_ref, p_ref, g_ref, pw_ref, ps_ref, wout_ref, npost_ref,
                o_ref, halo_ref):
    si = pl.program_id(1)

    @pl.when(si == 0)
    def _():
        halo_ref[...] = jnp.zeros_like(halo_ref)

    D = D_MODEL
    p = p_ref[0]
    ts = p.shape[0]
    e = jnp.concatenate([halo_ref[...], p], axis=0)
    halo_ref[...] = p[ts - POOL_HALO:ts, :]
    sums = []
    s_run = e
    shift = 1
    for _w in POOL_WINDOWS:
        s_run = s_run + pltpu.roll(s_run, shift, 0)
        sums.append(s_run[POOL_HALO:, :])
        shift *= 2
    pos = si * ts + lax.broadcasted_iota(jnp.int32, (ts, 1), 0) + 1
    lane = lax.broadcasted_iota(jnp.int32, (ts, POOL_W), 1)
    pooled = jnp.zeros((ts, POOL_W), F32)
    for gi, w in enumerate(POOL_WINDOWS):
        cnt = jnp.minimum(pos, w).astype(F32)
        grp = (lane >= POOL_GROUP_DIM * gi) & (lane < POOL_GROUP_DIM * (gi + 1))
        pooled = jnp.where(grp, sums[gi] / cnt, pooled)
    pooled = pooled - p
    y_pool = _dot(pooled.astype(BF16), pw_ref[...]) * ps_ref[...]

    y = jnp.concatenate([yf_ref[0].astype(F32), yr_ref[0], y_pool], axis=-1) * _silu(g_ref[0].astype(F32))
    t = _dot(y.astype(BF16), wout_ref[...])
    ms = jnp.mean(t * t, axis=-1, keepdims=True)
    tn = t * lax.rsqrt(ms + NORM_EPS) * npost_ref[...]
    gate = mod_ref[0][:, 2 * D:3 * D]
    o_ref[0] = x_ref[0] + gate * tn


def _out_proj(x, mod_l, y_fox, y_rwkv, p, g, pw_bd, pool_scale, w_out, npost, ts):
    B, S, D = x.shape
    ns = S // ts
    full = lambda a: pl.BlockSpec(a.shape, lambda b, s: (0,) * a.ndim)
    seq_spec = lambda w: pl.BlockSpec((1, ts, w), lambda b, s: (b, s, 0))
    return pl.pallas_call(
        _out_kernel,
        grid=(B, ns),
        in_specs=[seq_spec(D), pl.BlockSpec((1, 1, 3 * D), lambda b, s: (b, 0, 0)),
                  seq_spec(FOX_W), seq_spec(RWKV_W), seq_spec(POOL_W), seq_spec(D_MIX),
                  full(pw_bd), full(pool_scale), full(w_out), full(npost)],
        out_specs=seq_spec(D),
        out_shape=jax.ShapeDtypeStruct((B, S, D), F32),
        scratch_shapes=[pltpu.VMEM((POOL_HALO, POOL_W), F32)],
        compiler_params=_cparams(("arbitrary", "arbitrary")),
        name="out_proj",
    )(x, mod_l, y_fox, y_rwkv, p, g, pw_bd, pool_scale, w_out, npost)


def _block_diag(blocks):
    n = blocks.shape[0]
    d = blocks.shape[1]
    out = jnp.zeros((n * d, n * d), blocks.dtype)
    for i in range(n):
        out = out.at[i * d:(i + 1) * d, i * d:(i + 1) * d].set(blocks[i])
    return out


def kernel(x, c, ada_w, ada_b, norm_pre, norm_post, w_in, fox_q_gain, fox_k_gain, fox_f_bias, rwkv_mu, rwkv_w0, rwkv_w2, rwkv_a0, rwkv_a2, rwkv_k_k, rwkv_k_a, rwkv_r_k, rwkv_ln_g, rwkv_ln_b, pool_w, pool_scale, w_out):
    B, S, D = x.shape
    L = ada_w.shape[0]
    ts = min(512, S)
    tk = min(512, S // 2)
    cb = min(512, S)

    mod = _adaln_mod(c, ada_w, ada_b)
    o_q, o_k, o_v, o_f, o_z, o_p = (0, FOX_W, 2 * FOX_W, 3 * FOX_W, 3 * FOX_W + FOX_HEADS,
                                    3 * FOX_W + FOX_HEADS + RWKV_SHIFT_W)
    o_g = o_p + POOL_W
    row = lambda a: a.reshape(1, -1)

    for l in range(L):
        w = w_in[l]
        wq = w[:, o_q:o_k].astype(BF16)
        wk = w[:, o_k:o_v].astype(BF16)
        wv = w[:, o_v:o_f].astype(BF16)
        wf = jnp.pad(w[:, o_f:o_z], ((0, 0), (0, F_PAD - FOX_HEADS))).astype(BF16)
        wz = w[:, o_z:o_p].astype(BF16)
        wp = w[:, o_p:o_g].astype(BF16)
        wg = w[:, o_g:].astype(BF16)
        fbias = jnp.pad(fox_f_bias[l], (0, F_PAD - FOX_HEADS)).reshape(1, F_PAD)
        qg2 = jnp.tile(fox_q_gain[l], 2).reshape(1, LANES)
        kg2 = jnp.tile(fox_k_gain[l], 2).reshape(1, LANES)
        mod_l = mod[l].reshape(B, 1, 3 * D)

        q_aug, k_aug, v_aug, z, p, g = _in_proj(x, mod_l, row(norm_pre[l]), wq, wk, wv, wf, wz, wp, wg,
                                                fbias, qg2, kg2, ts)
        y_fox = _fox_attention(q_aug, k_aug, v_aug, tk)

        w2a2 = jnp.zeros((2 * HEAD_DIM, 2 * RWKV_W), F32)
        w2a2 = w2a2.at[:RWKV_DECAY_RANK, :RWKV_W].set(rwkv_w2[l])
        w2a2 = w2a2.at[RWKV_DECAY_RANK:, RWKV_W:].set(rwkv_a2[l])
        y_rwkv = _rwkv(z, row(rwkv_mu[l]), row(rwkv_w0[l]), row(rwkv_a0[l]), row(rwkv_k_k[l]),
                       row(rwkv_k_a[l]), row(rwkv_r_k[l]), row(rwkv_ln_g[l]), row(rwkv_ln_b[l]),
                       w2a2, cb)

        pw_bd = _block_diag(pool_w[l]).astype(BF16)
        x = _out_proj(x, mod_l, y_fox, y_rwkv, p, g, pw_bd, row(pool_scale[l]),
                      w_out[l].astype(BF16), row(norm_post[l]), ts)
    return x
```

```python
import functools
import math

import jax
import jax.numpy as jnp
from jax import lax
from jax.experimental import pallas as pl
from jax.experimental.pallas import tpu as pltpu

F32 = jnp.float32
BF16 = jnp.bfloat16

D_MODEL = 1024
HEAD_DIM = 64
FOX_HEADS = 8
FOX_W = FOX_HEADS * HEAD_DIM
RWKV_HEADS = 4
RWKV_W = RWKV_HEADS * HEAD_DIM
POOL_GROUPS = 4
POOL_GROUP_DIM = 64
POOL_W = POOL_GROUPS * POOL_GROUP_DIM
POOL_WINDOWS = (2, 4, 8, 16)
POOL_HALO = 16
D_MIX = FOX_W + RWKV_W + POOL_W
RWKV_DECAY_RANK = 64
RWKV_ICL_RANK = 64
RWKV_SHIFT_W = 3 * RWKV_W + RWKV_DECAY_RANK + RWKV_ICL_RANK
NORM_EPS = 1e-6
RWKV_GN_EPS = 64e-5
DECAY_SCALE = float(math.exp(-0.5))
LOG2E = float(1.0 / math.log(2.0))

LANES = 128
F_PAD = LANES
CHUNK = 64
NEG_BIG = -1e30

VMEM_LIMIT = 56 * 1024 * 1024


def _cparams(sem):
    return pltpu.CompilerParams(dimension_semantics=sem, vmem_limit_bytes=VMEM_LIMIT)


def _dot(a, b):
    return jnp.dot(a, b, preferred_element_type=F32)


def _dot_nt(a, b):
    return lax.dot_general(a, b, (((1,), (1,)), ((), ())), preferred_element_type=F32)


def _dot_tn(a, b):
    return lax.dot_general(a, b, (((0,), (0,)), ((), ())), preferred_element_type=F32)


def _split2(x):
    hi = x.astype(BF16)
    lo = (x - hi.astype(F32)).astype(BF16)
    return hi, lo


def _dot_wide_lhs(l_bf16, x):
    h, l = _split2(x)
    return _dot(l_bf16, h) + _dot(l_bf16, l)


def _dot_wide_rhs(x, r_bf16):
    h, l = _split2(x)
    return _dot(h, r_bf16) + _dot(l, r_bf16)


def _dot3(a, b, dot=_dot):
    ah, al = _split2(a)
    bh, bl = _split2(b)
    return dot(ah, bh) + dot(ah, bl) + dot(al, bh)


def _sigmoid(x):
    return 1.0 / (1.0 + jnp.exp(-x))


def _silu(x):
    return x * _sigmoid(x)


def _mod_kernel(c_ref, w_ref, b_ref, o_ref):
    sc = _silu(c_ref[...])
    o_ref[0] = jnp.dot(sc, w_ref[0], precision=lax.Precision.HIGHEST,
                       preferred_element_type=F32) + b_ref[0]


def _adaln_mod(c, ada_w, ada_b):
    L, D, D3 = ada_w.shape
    B = c.shape[0]
    nj = D3 // D
    return pl.pallas_call(
        _mod_kernel,
        grid=(L, nj),
        in_specs=[
            pl.BlockSpec((B, D), lambda l, j: (0, 0)),
            pl.BlockSpec((1, D, D), lambda l, j: (l, 0, j)),
            pl.BlockSpec((1, 1, D), lambda l, j: (l, 0, j)),
        ],
        out_specs=pl.BlockSpec((1, B, D), lambda l, j: (l, 0, j)),
        out_shape=jax.ShapeDtypeStruct((L, B, D3), F32),
        compiler_params=_cparams(("arbitrary", "arbitrary")),
        name="adaln_mod",
    )(c, ada_w, ada_b.reshape(L, 1, D3))


def _in_kernel(x_ref, mod_ref, npre_ref, wq_ref, wk_ref, wv_ref, wf_ref, wz_ref, wp_ref, wg_ref,
               fb_ref, qg_ref, kg_ref, ltri_ref,
               q_out, k_out, v_out, z_out, p_out, g_out, carry_ref):
    si = pl.program_id(1)

    @pl.when(si == 0)
    def _():
        carry_ref[...] = jnp.zeros_like(carry_ref)

    D = D_MODEL
    x = x_ref[0]
    ts = x.shape[0]
    ms = jnp.mean(x * x, axis=-1, keepdims=True)
    xn = x * lax.rsqrt(ms + NORM_EPS) * npre_ref[...]
    mod = mod_ref[0]
    shift = mod[:, 0:D]
    scale = mod[:, D:2 * D]
    h = (xn * (1.0 + scale) + shift).astype(BF16)

    f = _dot(h, wf_ref[...]) + fb_ref[...]
    logf = (jnp.minimum(f, 0.0) - jnp.log1p(jnp.exp(-jnp.abs(f)))) * LOG2E
    cum = _dot_wide_lhs(ltri_ref[...], logf) + carry_ref[...]
    carry_ref[...] = cum[ts - 1:ts, :]
    c1 = cum.astype(BF16).astype(F32)
    r1 = cum - c1
    c2 = r1.astype(BF16).astype(F32)
    c3 = r1 - c2

    uq = _dot(h, wq_ref[...])
    uk = _dot(h, wk_ref[...])
    uv = _dot(h, wv_ref[...])

    lane = lax.broadcasted_iota(jnp.int32, (ts, LANES), 1)
    lo = lane < HEAD_DIM
    qscale = (HEAD_DIM ** -0.5) * LOG2E

    def head_norm(x2, gain2):
        sq = x2 * x2
        ss_lo = jnp.sum(jnp.where(lo, sq, 0.0), axis=-1, keepdims=True)
        ss_hi = jnp.sum(jnp.where(lo, 0.0, sq), axis=-1, keepdims=True)
        inv = jnp.where(lo, lax.rsqrt(ss_lo * (1.0 / HEAD_DIM) + NORM_EPS),
                        lax.rsqrt(ss_hi * (1.0 / HEAD_DIM) + NORM_EPS))
        return x2 * inv * gain2

    def extras(e0, hcol, for_q):
        a1 = c1[:, hcol:hcol + 1]
        a2 = c2[:, hcol:hcol + 1]
        a3 = c3[:, hcol:hcol + 1]
        one = jnp.ones((ts, 1), F32)
        if for_q:
            vals = (a1, a2, a3, one, one, one)
        else:
            vals = (one, one, one, -a1, -a2, -a3)
        e = jnp.zeros((ts, LANES), F32)
        for i, val in enumerate(vals):
            e = jnp.where(lane == e0 + i, val, e)
        return e

    for j in range(FOX_HEADS // 2):
        sl = slice(LANES * j, LANES * (j + 1))
        qn = head_norm(uq[:, sl], qg_ref[...]) * qscale
        kn = head_norm(uk[:, sl], kg_ref[...])
        v2 = uv[:, sl]
        he, ho = 2 * j, 2 * j + 1
        q_out[0, he] = jnp.where(lo, qn, extras(HEAD_DIM, he, True)).astype(BF16)
        k_out[0, he] = jnp.where(lo, kn, extras(HEAD_DIM, he, False)).astype(BF16)
        v_out[0, he] = jnp.where(lo, v2, jnp.where(lane == HEAD_DIM, 1.0, 0.0)).T.astype(BF16)
        q_out[0, ho] = jnp.where(lo, extras(0, ho, True), qn).astype(BF16)
        k_out[0, ho] = jnp.where(lo, extras(0, ho, False), kn).astype(BF16)
        v_out[0, ho] = jnp.where(lo, jnp.where(lane == 0, 1.0, 0.0), v2).T.astype(BF16)

    z_out[0] = _dot(h, wz_ref[...])
    p_out[0] = _dot(h, wp_ref[...])
    g_out[0] = _dot(h, wg_ref[...]).astype(BF16)


def _in_proj(x, mod_l, npre, wq, wk, wv, wf, wz, wp, wg, fbias, qg2, kg2, ts):
    B, S, D = x.shape
    ns = S // ts
    ltri = jnp.tril(jnp.ones((ts, ts), F32)).astype(BF16)
    full = lambda a: pl.BlockSpec(a.shape, lambda b, s: (0,) * a.ndim)
    hs = jax.ShapeDtypeStruct((B, FOX_HEADS, S, LANES), BF16)
    head_spec = pl.BlockSpec((1, FOX_HEADS, ts, LANES), lambda b, s: (b, 0, s, 0))
    seq_spec = lambda w: pl.BlockSpec((1, ts, w), lambda b, s: (b, s, 0))
    return pl.pallas_call(
        _in_kernel,
        grid=(B, ns),
        in_specs=[
            seq_spec(D),
            pl.BlockSpec((1, 1, 3 * D), lambda b, s: (b, 0, 0)),
            full(npre), full(wq), full(wk), full(wv), full(wf), full(wz), full(wp), full(wg),
            full(fbias), full(qg2), full(kg2), full(ltri),
        ],
        out_specs=[head_spec, head_spec,
                   pl.BlockSpec((1, FOX_HEADS, LANES, ts), lambda b, s: (b, 0, 0, s)),
                   seq_spec(RWKV_SHIFT_W), seq_spec(POOL_W), seq_spec(D_MIX)],
        out_shape=[hs, hs, jax.ShapeDtypeStruct((B, FOX_HEADS, LANES, S), BF16),
                   jax.ShapeDtypeStruct((B, S, RWKV_SHIFT_W), F32),
                   jax.ShapeDtypeStruct((B, S, POOL_W), F32),
                   jax.ShapeDtypeStruct((B, S, D_MIX), BF16)],
        scratch_shapes=[pltpu.VMEM((1, LANES), F32)],
        compiler_params=_cparams(("arbitrary", "arbitrary")),
        name="in_proj",
    )(x, mod_l, npre, wq, wk, wv, wf, wz, wp, wg, fbias, qg2, kg2, ltri)


def _attn_kernel(q_ref, k_ref, vt_ref, o_ref, *, tk):
    qi = pl.program_id(2)
    tq = 2 * tk
    heads = range(2)

    def scores_t(hh, q, blk, w):
        off = pl.multiple_of(blk * w, w)
        return _dot_nt(k_ref[0, hh, pl.ds(off, w), :], q)

    def update(hh, st, blk, w, m, acc):
        off = pl.multiple_of(blk * w, w)
        m_new = jnp.maximum(m, jnp.max(st, axis=0, keepdims=True))
        alpha = jnp.exp2(m - m_new)
        p = jnp.exp2(st - m_new).astype(BF16)
        acc = alpha * acc + _dot(vt_ref[0, hh, :, pl.ds(off, w)], p)
        return m_new, acc

    m0 = jnp.full((1, tq), NEG_BIG, F32)
    a0 = jnp.zeros((LANES, tq), F32)

    def body(j, carry):
        ms, accs_ = carry
        ss = [scores_t(hh, q_ref[0, hh], j, tq) for hh in heads]
        upd = [update(hh, ss[hh], j, tq, ms[hh], accs_[hh]) for hh in heads]
        return tuple(u[0] for u in upd), tuple(u[1] for u in upd)

    ms, accs = lax.fori_loop(0, qi, body, ((m0, m0), (a0, a0)))

    key_u = lax.broadcasted_iota(jnp.int32, (tk, tk), 0)
    qry_u = lax.broadcasted_iota(jnp.int32, (tk, tk), 1)
    key_l = lax.broadcasted_iota(jnp.int32, (tq, tk), 0)
    qry_l = lax.broadcasted_iota(jnp.int32, (tq, tk), 1)
    s_up = [jnp.where(key_u <= qry_u, scores_t(hh, q_ref[0, hh, 0:tk, :], 2 * qi, tk), NEG_BIG) for hh in heads]
    s_lo = [jnp.where(key_l <= qry_l + tk, scores_t(hh, q_ref[0, hh, tk:tq, :], qi, tq), NEG_BIG) for hh in heads]
    up = [update(hh, s_up[hh], 2 * qi, tk, ms[hh][:, 0:tk], accs[hh][:, 0:tk])[1] for hh in heads]
    low = [update(hh, s_lo[hh], qi, tq, ms[hh][:, tk:tq], accs[hh][:, tk:tq])[1] for hh in heads]
    accs = [jnp.concatenate([up[hh], low[hh]], axis=1) for hh in heads]

    l_e = accs[0][HEAD_DIM:HEAD_DIM + 1, :]
    l_o = accs[1][0:1, :]
    chan = lax.broadcasted_iota(jnp.int32, (LANES, tq), 0)
    o_t = jnp.where(chan < HEAD_DIM, accs[0] / l_e, accs[1] / l_o)
    o_ref[0] = o_t.T.astype(o_ref.dtype)


def _fox_attention(q, k, v, tk):
    B, H, S, _ = q.shape
    tq = 2 * tk
    nq = S // tq
    return pl.pallas_call(
        functools.partial(_attn_kernel, tk=tk),
        grid=(B, H // 2, nq),
        in_specs=[
            pl.BlockSpec((1, 2, tq, LANES), lambda b, p, i: (b, p, i, 0)),
            pl.BlockSpec((1, 2, S, LANES), lambda b, p, i: (b, p, 0, 0)),
            pl.BlockSpec((1, 2, LANES, S), lambda b, p, i: (b, p, 0, 0)),
        ],
        out_specs=pl.BlockSpec((1, tq, LANES), lambda b, p, i: (b, i, p)),
        out_shape=jax.ShapeDtypeStruct((B, S, FOX_W), BF16),
        compiler_params=_cparams(("arbitrary", "arbitrary", "arbitrary")),
        name="fox_attention",
    )(q, k, v)


def _mm(a, b):
    return _dot(a.astype(BF16), b.astype(BF16))


def _mm_nt(a, b):
    return _dot_nt(a.astype(BF16), b.astype(BF16))


def _mm_tn(a, b):
    return _dot_tn(a.astype(BF16), b.astype(BF16))


def _unit_lower_inverses(ms, row, col):
    eye = (row == col).astype(F32)
    same = lambda sh: (row >> sh) == (col >> sh)
    ts = [eye + jnp.where(same(1), m, 0.0) for m in ms]
    for sh in range(1, 6):
        band = jnp.logical_and(same(sh + 1), jnp.logical_not(same(sh)))
        ets = [_mm(jnp.where(band, m, 0.0), t) for m, t in zip(ms, ts)]
        ts = [t + _mm(t, et) for t, et in zip(ts, ets)]
    return ts


def _rwkv_kernel(z_ref, mu_ref, w0_ref, a0_ref, kk_ref, ka_ref, rk_ref, lng_ref, lnb_ref,
                 w2a2_ref, seg_ref, ltri_ref, y_ref, state_ref, prev_ref):
    si = pl.program_id(1)

    @pl.when(si == 0)
    def _():
        state_ref[...] = jnp.zeros_like(state_ref)
        prev_ref[...] = jnp.zeros_like(prev_ref)

    W = RWKV_W
    N = HEAD_DIM
    z = z_ref[0]
    cb = z.shape[0]
    row = lax.broadcasted_iota(jnp.int32, (cb, 1), 0)
    zprev = jnp.where(row == 0, prev_ref[...], pltpu.roll(z, 1, 0))
    prev_ref[...] = z[cb - 1:cb, :]
    zf = z + (zprev - z) * mu_ref[...]
    r = zf[:, 0:W]
    k = zf[:, W:2 * W]
    v = zf[:, 2 * W:3 * W]
    t6 = zf[:, 3 * W:3 * W + 2 * N]
    lane = lax.broadcasted_iota(jnp.int32, (cb, 2 * N), 1)
    tz = jnp.where(lane < N, jnp.tanh(t6), t6)
    dwa = _dot3(tz, w2a2_ref[...])
    logw = -DECAY_SCALE * _sigmoid(w0_ref[...] + dwa[:, 0:W])
    a = _sigmoid(a0_ref[...] + dwa[:, W:2 * W])

    seg = seg_ref[...]
    kkr = k * kk_ref[...]
    kk = kkr / jnp.maximum(jnp.sqrt(_dot_wide_rhs(kkr * kkr, seg)), 1e-12)
    k2 = k * (1.0 + (a - 1.0) * ka_ref[...])
    bonus = _dot_wide_rhs(r * k2 * rk_ref[...], seg) * v

    logg = _dot_wide_lhs(ltri_ref[...], logw)
    g = jnp.exp(logg)
    gex = jnp.exp(logg - logw)
    gi = jnp.exp(-logg)
    at = -kk * gex
    rt = r * g
    bt = kk * a * gi
    kt = k2 * gi

    rw = lax.broadcasted_iota(jnp.int32, (W, W), 0)
    cw = lax.broadcasted_iota(jnp.int32, (W, W), 1)
    head_blk = (rw >> 6) == (cw >> 6)
    tw = lax.broadcasted_iota(jnp.int32, (CHUNK, W), 0)
    sw = lax.broadcasted_iota(jnp.int32, (CHUNK, W), 1) & (CHUNK - 1)
    strict_w = sw < tw
    incl_w = sw <= tw

    def tall_f32(zc):
        return jnp.where(head_blk, jnp.concatenate([zc] * RWKV_HEADS, axis=0), 0.0)

    def tall(zc):
        return tall_f32(zc).astype(BF16)

    def fold(y):
        return y[0:CHUNK] + y[CHUNK:2 * CHUNK] + y[2 * CHUNK:3 * CHUNK] + y[3 * CHUNK:4 * CHUNK]

    nchunk = cb // CHUNK
    chunks = range(nchunk)
    rows = [slice(CHUNK * c, CHUNK * (c + 1)) for c in chunks]
    glast = [g[CHUNK * (c + 1) - 1:CHUNK * (c + 1), :] for c in chunks]
    t_a = [tall(at[rs]) for rs in rows]
    t_bk = [jnp.concatenate([tall(bt[rs]), tall(kt[rs])], axis=0) for rs in rows]
    t_v = [tall(v[rs]) for rs in rows]
    pp = [_dot_nt(jnp.concatenate([at[rows[c]], rt[rows[c]]], axis=0).astype(BF16), t_bk[c]) for c in chunks]
    m_ab = [tall_f32(jnp.where(strict_w, p[0:CHUNK, 0:W], 0.0)) for p in pp]
    m_akrk = [jnp.concatenate([jnp.where(strict_w, p[0:CHUNK, W:2 * W], 0.0),
                               jnp.where(incl_w, p[CHUNK:2 * CHUNK, W:2 * W], 0.0)], axis=0) for p in pp]
    m_rb = [jnp.where(incl_w, p[CHUNK:2 * CHUNK, 0:W], 0.0).astype(BF16) for p in pp]
    xo = [_mm(m_akrk[c], t_v[c]) for c in chunks]
    d0 = [jnp.where(head_blk, _mm_tn(v[rows[c]], kt[rows[c]] * glast[c]), 0.0) for c in chunks]
    bh = [(bt[rows[c]] * glast[c]).astype(BF16) for c in chunks]
    tinv = _unit_lower_inverses(m_ab, rw, cw)
    wu = [_mm(fold(tinv[c]), jnp.concatenate([t_a[c], tall(xo[c][0:CHUNK])], axis=1)) for c in chunks]
    w_all = [x[:, 0:W] for x in wu]
    u0_all = [x[:, W:2 * W] for x in wu]
    o0_all = [x[CHUNK:2 * CHUNK] for x in xo]
    pq = [_dot_tn(wu[c].astype(BF16), bh[c]) for c in chunks]
    phi = [jnp.where(head_blk, x[0:W], 0.0).astype(BF16) for x in pq]
    psi = [jnp.where(head_blk, pq[c][W:2 * W], 0.0) + d0[c] for c in chunks]
    mw = [_dot(m_rb[c], jnp.concatenate([tall(w_all[c]), tall(u0_all[c])], axis=1)) for c in chunks]
    omega = [(rt[rows[c]] + mw[c][:, 0:W]).astype(BF16) for c in chunks]
    xi = [mw[c][:, W:2 * W] + o0_all[c] for c in chunks]

    outs = []
    s_bd = state_ref[...]
    for c in chunks:
        s_bf = s_bd.astype(BF16)
        outs.append(_dot_nt(omega[c], s_bf) + xi[c])
        s_bd = s_bd * glast[c] + _dot(s_bf, phi[c]) + psi[c]
    state_ref[...] = s_bd

    o = jnp.concatenate(outs, axis=0)
    mean = _dot_wide_rhs(o, seg) * (1.0 / N)
    d = o - mean
    var = _dot_wide_rhs(d * d, seg) * (1.0 / N)
    y_ref[0] = d * lax.rsqrt(var + RWKV_GN_EPS) * lng_ref[...] + lnb_ref[...] + bonus


def _rwkv(z, mu, w0, a0, k_k, k_a, r_k, ln_g, ln_b, w2a2, cb):
    B, S, _ = z.shape
    ns = S // cb
    seg = jnp.kron(jnp.eye(RWKV_HEADS, dtype=F32), jnp.ones((HEAD_DIM, HEAD_DIM), F32)).astype(BF16)
    ltri = jnp.kron(jnp.eye(cb // CHUNK, dtype=F32), jnp.tril(jnp.ones((CHUNK, CHUNK), F32))).astype(BF16)
    full = lambda a: pl.BlockSpec(a.shape, lambda b, s: (0,) * a.ndim)
    args = (mu, w0, a0, k_k, k_a, r_k, ln_g, ln_b, w2a2, seg, ltri)
    return pl.pallas_call(
        _rwkv_kernel,
        grid=(B, ns),
        in_specs=[pl.BlockSpec((1, cb, RWKV_SHIFT_W), lambda b, s: (b, s, 0))] + [full(a) for a in args],
        out_specs=pl.BlockSpec((1, cb, RWKV_W), lambda b, s: (b, s, 0)),
        out_shape=jax.ShapeDtypeStruct((B, S, RWKV_W), F32),
        scratch_shapes=[pltpu.VMEM((RWKV_W, RWKV_W), F32),
                        pltpu.VMEM((1, RWKV_SHIFT_W), F32)],
        compiler_params=_cparams(("arbitrary", "arbitrary")),
        name="rwkv7",
    )(z, *args)


def _out_kernel(x_ref, mod_ref, yf_ref, yr_ref, p_ref, g_ref, pw_ref, ps_ref, wout_ref, npost_ref,
                o_ref, halo_ref):
    si = pl.program_id(1)

    @pl.when(si == 0)
    def _():
        halo_ref[...] = jnp.zeros_like(halo_ref)

    D = D_MODEL
    p = p_ref[0]
    ts = p.shape[0]
    e = jnp.concatenate([halo_ref[...], p], axis=0)
    halo_ref[...] = p[ts - POOL_HALO:ts, :]
    sums = []
    s_run = e
    shift = 1
    for _w in POOL_WINDOWS:
        s_run = s_run + pltpu.roll(s_run, shift, 0)
        sums.append(s_run[POOL_HALO:, :])
        shift *= 2
    pos = si * ts + lax.broadcasted_iota(jnp.int32, (ts, 1), 0) + 1
    lane = lax.broadcasted_iota(jnp.int32, (ts, POOL_W), 1)
    pooled = jnp.zeros((ts, POOL_W), F32)
    for gi, w in enumerate(POOL_WINDOWS):
        cnt = jnp.minimum(pos, w).astype(F32)
        grp = (lane >= POOL_GROUP_DIM * gi) & (lane < POOL_GROUP_DIM * (gi + 1))
        pooled = jnp.where(grp, sums[gi] / cnt, pooled)
    pooled = pooled - p
    y_pool = _dot(pooled.astype(BF16), pw_ref[...]) * ps_ref[...]

    y = jnp.concatenate([yf_ref[0].astype(F32), yr_ref[0], y_pool], axis=-1) * _silu(g_ref[0].astype(F32))
    t = _dot(y.astype(BF16), wout_ref[...])
    ms = jnp.mean(t * t, axis=-1, keepdims=True)
    tn = t * lax.rsqrt(ms + NORM_EPS) * npost_ref[...]
    gate = mod_ref[0][:, 2 * D:3 * D]
    o_ref[0] = x_ref[0] + gate * tn


def _out_proj(x, mod_l, y_fox, y_rwkv, p, g, pw_bd, pool_scale, w_out, npost, ts):
    B, S, D = x.shape
    ns = S // ts
    full = lambda a: pl.BlockSpec(a.shape, lambda b, s: (0,) * a.ndim)
    seq_spec = lambda w: pl.BlockSpec((1, ts, w), lambda b, s: (b, s, 0))
    return pl.pallas_call(
        _out_kernel,
        grid=(B, ns),
        in_specs=[seq_spec(D), pl.BlockSpec((1, 1, 3 * D), lambda b, s: (b, 0, 0)),
                  seq_spec(FOX_W), seq_spec(RWKV_W), seq_spec(POOL_W), seq_spec(D_MIX),
                  full(pw_bd), full(pool_scale), full(w_out), full(npost)],
        out_specs=seq_spec(D),
        out_shape=jax.ShapeDtypeStruct((B, S, D), F32),
        scratch_shapes=[pltpu.VMEM((POOL_HALO, POOL_W), F32)],
        compiler_params=_cparams(("arbitrary", "arbitrary")),
        name="out_proj",
    )(x, mod_l, y_fox, y_rwkv, p, g, pw_bd, pool_scale, w_out, npost)


def _block_diag(blocks):
    n = blocks.shape[0]
    d = blocks.shape[1]
    out = jnp.zeros((n * d, n * d), blocks.dtype)
    for i in range(n):
        out = out.at[i * d:(i + 1) * d, i * d:(i + 1) * d].set(blocks[i])
    return out


def kernel(x, c, ada_w, ada_b, norm_pre, norm_post, w_in, fox_q_gain, fox_k_gain, fox_f_bias, rwkv_mu, rwkv_w0, rwkv_w2, rwkv_a0, rwkv_a2, rwkv_k_k, rwkv_k_a, rwkv_r_k, rwkv_ln_g, rwkv_ln_b, pool_w, pool_scale, w_out):
    B, S, D = x.shape
    L = ada_w.shape[0]
    ts = min(512, S)
    tk = min(512, S // 2)
    cb = min(512, S)

    mod = _adaln_mod(c, ada_w, ada_b)
    o_q, o_k, o_v, o_f, o_z, o_p = (0, FOX_W, 2 * FOX_W, 3 * FOX_W, 3 * FOX_W + FOX_HEADS,
                                    3 * FOX_W + FOX_HEADS + RWKV_SHIFT_W)
    o_g = o_p + POOL_W
    row = lambda a: a.reshape(1, -1)

    for l in range(L):
        w = w_in[l]
        wq = w[:, o_q:o_k].astype(BF16)
        wk = w[:, o_k:o_v].astype(BF16)
        wv = w[:, o_v:o_f].astype(BF16)
        wf = jnp.pad(w[:, o_f:o_z], ((0, 0), (0, F_PAD - FOX_HEADS))).astype(BF16)
        wz = w[:, o_z:o_p].astype(BF16)
        wp = w[:, o_p:o_g].astype(BF16)
        wg = w[:, o_g:].astype(BF16)
        fbias = jnp.pad(fox_f_bias[l], (0, F_PAD - FOX_HEADS)).reshape(1, F_PAD)
        qg2 = jnp.tile(fox_q_gain[l], 2).reshape(1, LANES)
        kg2 = jnp.tile(fox_k_gain[l], 2).reshape(1, LANES)
        mod_l = mod[l].reshape(B, 1, 3 * D)

        q_aug, k_aug, v_aug, z, p, g = _in_proj(x, mod_l, row(norm_pre[l]), wq, wk, wv, wf, wz, wp, wg,
                                                fbias, qg2, kg2, ts)
        y_fox = _fox_attention(q_aug, k_aug, v_aug, tk)

        w2a2 = jnp.zeros((2 * HEAD_DIM, 2 * RWKV_W), F32)
        w2a2 = w2a2.at[:RWKV_DECAY_RANK, :RWKV_W].set(rwkv_w2[l])
        w2a2 = w2a2.at[RWKV_DECAY_RANK:, RWKV_W:].set(rwkv_a2[l])
        y_rwkv = _rwkv(z, row(rwkv_mu[l]), row(rwkv_w0[l]), row(rwkv_a0[l]), row(rwkv_k_k[l]),
                       row(rwkv_k_a[l]), row(rwkv_r_k[l]), row(rwkv_ln_g[l]), row(rwkv_ln_b[l]),
                       w2a2, cb)

        pw_bd = _block_diag(pool_w[l]).astype(BF16)
        x = _out_proj(x, mod_l, y_fox, y_rwkv, p, g, pw_bd, row(pool_scale[l]),
                      w_out[l].astype(BF16), row(norm_post[l]), ts)
    return x
```
